```python
import math
import jax, jax.numpy as jnp
from jax import lax
import numpy as np

D_MODEL = 2048
BATCH = 2
SEQ = 4096
DEPTH = 1

CHUNK = 64
LN_EPS = 1e-5
D_LRU = 1024
N_LRU_BLOCKS = 8
LRU_BLOCK = D_LRU // N_LRU_BLOCKS
CONV_W = 4
LRU_C = 8.0
N_HEADS = 8
HEAD_DIM = 128
D_ATT = N_HEADS * HEAD_DIM
N_IDX_HEADS = 16
IDX_DIM = 64
TOPK_MAX = 256
Q_BLOCK = 128
ROPE_THETA = 500000.0
ROT_ATT = HEAD_DIM // 4
ROT_IDX = IDX_DIM // 4
D_MIX = D_LRU + D_ATT
IN_SPLITS = (D_LRU, D_LRU, D_ATT, D_ATT, D_ATT, N_IDX_HEADS * IDX_DIM, IDX_DIM, N_IDX_HEADS)
D_IN = sum(IN_SPLITS)
PEER_HEADS = 8
N_KEYS = 128
N_EXPERTS = N_KEYS * N_KEYS
PEER_KEY_DIM = 256
PEER_TOPK = 16
TOK_BLOCK = 128
DEEPNORM_ALPHA = (2.0 * DEPTH) ** 0.25
DEEPNORM_BETA = (8.0 * DEPTH) ** -0.25

kernel_name = "hybrid_rglru_dsa_peer_deepnorm_adaln"


def layer_norm(x):
    xf = x.astype(jnp.float32)
    mu = jnp.mean(xf, -1, keepdims=True)
    var = jnp.mean(jnp.square(xf - mu), -1, keepdims=True)
    return ((xf - mu) * lax.rsqrt(var + LN_EPS)).astype(x.dtype)


def partial_rope(x, pos, rot_dim):
    half = rot_dim // 2
    inv_freq = ROPE_THETA ** (-jnp.arange(half, dtype=jnp.float32) / half)
    ang = pos.astype(jnp.float32)[:, :, None] * inv_freq
    cos = jnp.cos(ang)[:, :, None, :].astype(x.dtype)
    sin = jnp.sin(ang)[:, :, None, :].astype(x.dtype)
    x1, x2, xp = x[..., :half], x[..., half:rot_dim], x[..., rot_dim:]
    return jnp.concatenate([x1 * cos - x2 * sin, x2 * cos + x1 * sin, xp], -1)


def rglru_group(xb, gb, conv_w, conv_b, w_rg_a, b_rg_a, w_rg_x, b_rg_x, lam):
    B, S, _ = xb.shape
    xpad = jnp.pad(xb, ((0, 0), (CONV_W - 1, 0), (0, 0)))
    xc = sum(xpad[:, j:j + S] * conv_w[j] for j in range(CONV_W)) + conv_b
    xblk = xc.reshape(B, S, N_LRU_BLOCKS, LRU_BLOCK)
    r = jax.nn.sigmoid(jnp.einsum('bsnd,nde->bsne', xblk, w_rg_a).reshape(B, S, D_LRU) + b_rg_a)
    i = jax.nn.sigmoid(jnp.einsum('bsnd,nde->bsne', xblk, w_rg_x).reshape(B, S, D_LRU) + b_rg_x)
    log_a = -LRU_C * r.astype(jnp.float32) * jax.nn.softplus(-lam.astype(jnp.float32))
    a = jnp.exp(log_a)
    b = jnp.sqrt(-jnp.expm1(2.0 * log_a)) * (i * xc).astype(jnp.float32)

    def combine(e1, e2):
        a1, b1 = e1
        a2, b2 = e2
        return a1 * a2, a2 * b1 + b2

    _, h = lax.associative_scan(combine, (a, b), axis=1)
    return h.astype(xb.dtype) * jax.nn.gelu(gb)


def dsa_group(q, k, v, q_idx, k_idx, w_idx, k_idx_g, pos):
    B, S, _ = q.shape
    topk = min(TOPK_MAX, S // 4)
    q = partial_rope(q.reshape(B, S, N_HEADS, HEAD_DIM), pos, ROT_ATT)
    k = partial_rope(k.reshape(B, S, N_HEADS, HEAD_DIM), pos, ROT_ATT)
    v = v.reshape(B, S, N_HEADS, HEAD_DIM)
    q_idx = partial_rope(q_idx.reshape(B, S, N_IDX_HEADS, IDX_DIM), pos, ROT_IDX)
    k_idx = partial_rope((layer_norm(k_idx) * k_idx_g)[:, :, None, :], pos, ROT_IDX)[:, :, 0]
    w_idx = w_idx * (N_IDX_HEADS ** -0.5)
    key_chunk = jnp.arange(S) // CHUNK
    idx_scale = IDX_DIM ** -0.5
    att_scale = HEAD_DIM ** -0.5

    def block(bi):
        start = bi * Q_BLOCK
        qib = lax.dynamic_slice_in_dim(q_idx, start, Q_BLOCK, axis=1)
        wib = lax.dynamic_slice_in_dim(w_idx, start, Q_BLOCK, axis=1)
        logits = jnp.einsum('bthd,bsd->bths', qib, k_idx).astype(jnp.float32)
        score = jnp.einsum('bth,bths->bts', wib.astype(jnp.float32), jax.nn.relu(logits) * idx_scale)
        q_chunk = (start + jnp.arange(Q_BLOCK)) // CHUNK
        adm = key_chunk[None, :] <= q_chunk[:, None]
        score = jnp.where(adm[None], score, -jnp.inf)
        _, sel = lax.top_k(score, topk)
        valid = key_chunk[sel] <= q_chunk[None, :, None]
        k_sel = jax.vmap(lambda kk, ii: kk[ii])(k, sel)
        v_sel = jax.vmap(lambda vv, ii: vv[ii])(v, sel)
        qb = lax.dynamic_slice_in_dim(q, start, Q_BLOCK, axis=1)
        s = jnp.einsum('bthd,btkhd->bhtk', qb, k_sel).astype(jnp.float32) * att_scale
        s = jnp.where(valid[:, None], s, -jnp.inf)
        p = jax.nn.softmax(s, axis=-1).astype(v.dtype)
        return jnp.einsum('bhtk,btkhd->bthd', p, v_sel)

    out = lax.map(block, jnp.arange(S // Q_BLOCK))
    return out.transpose(1, 0, 2, 3, 4).reshape(B, S, D_ATT)


def peer(u, w_pq, sub_keys, u_tab, v_tab):
    B, S, D = u.shape
    q = (u @ w_pq).reshape(B, S, PEER_HEADS, 2, PEER_KEY_DIM // 2)
    s = jnp.einsum('bshpd,hpnd->bshpn', q, sub_keys).astype(jnp.float32)
    s1, i1 = lax.top_k(s[..., 0, :], PEER_TOPK)
    s2, i2 = lax.top_k(s[..., 1, :], PEER_TOPK)
    cand_s = (s1[..., :, None] + s2[..., None, :]).reshape(B, S, PEER_HEADS, PEER_TOPK * PEER_TOPK)
    cand_i = (i1[..., :, None] * N_KEYS + i2[..., None, :]).reshape(B, S, PEER_HEADS, PEER_TOPK * PEER_TOPK)
    top_s, pick = lax.top_k(cand_s, PEER_TOPK)
    eidx = jnp.take_along_axis(cand_i, pick, axis=-1)
    g = jax.nn.softmax(top_s, axis=-1)
    nb = (B * S) // TOK_BLOCK
    uf = u.reshape(nb, TOK_BLOCK, D)
    ef = eidx.reshape(nb, TOK_BLOCK, PEER_HEADS, PEER_TOPK)
    gf = g.reshape(nb, TOK_BLOCK, PEER_HEADS, PEER_TOPK)

    def apply(args):
        xb, eb, gb = args
        act = jax.nn.gelu(jnp.einsum('thkd,td->thk', u_tab[eb], xb).astype(jnp.float32))
        hk = (gb * act).astype(xb.dtype)
        return jnp.einsum('thk,thkd->td', hk, v_tab[eb])

    return lax.map(apply, (uf, ef, gf)).reshape(B, S, D)


def setup_inputs(seed: int = 0) -> dict:
    key = jax.random.key(seed)
    ks = jax.random.split(key, 24)
    f32 = jnp.float32
    L = DEPTH

    def nrm(k, shape, scale):
        return jax.random.normal(k, shape, f32) * scale

    x = nrm(ks[0], (BATCH, SEQ, D_MODEL), 1.0)
    c = nrm(ks[1], (BATCH, D_MODEL), 1.0)
    offs = jax.random.randint(ks[2], (BATCH, 1), 0, 1024, dtype=jnp.int32)
    positions = (offs + jnp.arange(SEQ, dtype=jnp.int32)[None, :]).astype(jnp.int32)
    w_ada = nrm(ks[3], (L, D_MODEL, 6 * D_MODEL), 0.2 * D_MODEL ** -0.5)
    b_ada = nrm(ks[4], (L, 6 * D_MODEL), 0.02)
    w_in = nrm(ks[5], (L, D_MODEL, D_IN), D_MODEL ** -0.5)
    conv_w = nrm(ks[6], (L, CONV_W, D_LRU), CONV_W ** -0.5)
    conv_b = nrm(ks[7], (L, D_LRU), 0.02)
    w_rg_a = nrm(ks[8], (L, N_LRU_BLOCKS, LRU_BLOCK, LRU_BLOCK), LRU_BLOCK ** -0.5)
    b_rg_a = nrm(ks[9], (L, D_LRU), 0.02)
    w_rg_x = nrm(ks[10], (L, N_LRU_BLOCKS, LRU_BLOCK, LRU_BLOCK), LRU_BLOCK ** -0.5)
    b_rg_x = nrm(ks[11], (L, D_LRU), 0.02)
    a_c = jax.random.uniform(ks[12], (L, D_LRU), f32, 0.9, 0.999)
    sg = a_c ** (1.0 / LRU_C)
    lru_lambda = jnp.log(sg) - jnp.log1p(-sg)
    k_idx_g = 1.0 + nrm(ks[13], (L, IDX_DIM), 0.02)
    w_out = nrm(ks[14], (L, D_MIX, D_MODEL), DEEPNORM_BETA * D_MIX ** -0.5)
    ln1_g = 1.0 + nrm(ks[15], (L, D_MODEL), 0.02)
    ln1_b = nrm(ks[16], (L, D_MODEL), 0.02)
    w_pq = nrm(ks[17], (L, D_MODEL, PEER_HEADS * PEER_KEY_DIM), D_MODEL ** -0.5)
    peer_sub_keys = nrm(ks[18], (L, PEER_HEADS, 2, N_KEYS, PEER_KEY_DIM // 2), (PEER_KEY_DIM // 2) ** -0.5)
    peer_u = nrm(ks[19], (L, N_EXPERTS, D_MODEL), D_MODEL ** -0.5)
    peer_v = nrm(ks[20], (L, N_EXPERTS, D_MODEL), DEEPNORM_BETA * PEER_HEADS ** -0.5)
    ln2_g = 1.0 + nrm(ks[21], (L, D_MODEL), 0.02)
    ln2_b = nrm(ks[22], (L, D_MODEL), 0.02)
    return {"x": x, "c": c, "positions": positions, "w_ada": w_ada, "b_ada": b_ada,
            "w_in": w_in, "conv_w": conv_w, "conv_b": conv_b, "w_rg_a": w_rg_a,
            "b_rg_a": b_rg_a, "w_rg_x": w_rg_x, "b_rg_x": b_rg_x, "lru_lambda": lru_lambda,
            "k_idx_g": k_idx_g, "w_out": w_out, "ln1_g": ln1_g, "ln1_b": ln1_b,
            "w_pq": w_pq, "peer_sub_keys": peer_sub_keys, "peer_u": peer_u, "peer_v": peer_v,
            "ln2_g": ln2_g, "ln2_b": ln2_b}


def reference(x, c, positions, w_ada, b_ada, w_in, conv_w, conv_b, w_rg_a, b_rg_a, w_rg_x,
              b_rg_x, lru_lambda, k_idx_g, w_out, ln1_g, ln1_b, w_pq, peer_sub_keys, peer_u,
              peer_v, ln2_g, ln2_b):
    split_at = np.cumsum(IN_SPLITS)[:-1].tolist()
    for l in range(DEPTH):
        mod = jax.nn.silu(c) @ w_ada[l] + b_ada[l]
        sh1, sc1, gt1, sh2, sc2, gt2 = [m[:, None, :] for m in jnp.split(mod, 6, axis=-1)]
        u1 = layer_norm(x) * (1.0 + sc1) + sh1
        xl, gl, q, k, v, qi, ki, wi = jnp.split(u1 @ w_in[l], split_at, axis=-1)
        y_lru = rglru_group(xl, gl, conv_w[l], conv_b[l], w_rg_a[l], b_rg_a[l], w_rg_x[l],
                            b_rg_x[l], lru_lambda[l])
        y_att = dsa_group(q, k, v, qi, ki, wi, k_idx_g[l], positions)
        mix = jnp.concatenate([y_lru, y_att], axis=-1) @ w_out[l]
        x = layer_norm(DEEPNORM_ALPHA * x + (1.0 + gt1) * mix) * ln1_g[l] + ln1_b[l]
        u2 = layer_norm(x) * (1.0 + sc2) + sh2
        y_ffn = peer(u2, w_pq[l], peer_sub_keys[l], peer_u[l], peer_v[l])
        x = layer_norm(DEEPNORM_ALPHA * x + (1.0 + gt2) * y_ffn) * ln2_g[l] + ln2_b[l]
    return x
```

```python
import functools
import math

import jax
import jax.numpy as jnp
import numpy as np
from jax import lax
from jax.experimental import pallas as pl
from jax.experimental.pallas import tpu as pltpu

F32 = jnp.float32
BF16 = jnp.bfloat16
I32 = jnp.int32

LANES = 128
SUBLANES = 8
VMEM_LIMIT = 56 * 1024 * 1024

LN_EPS = 1e-5
CHUNK = 64
CHUNK_SHIFT = 6
assert 1 << CHUNK_SHIFT == CHUNK
CONV_W = 4
LRU_C = 8.0
HEAD_DIM = 128
IDX_DIM = 64
TOPK_MAX = 256
Q_BLOCK = 128
ROPE_THETA = 500000.0
ROT_ATT = HEAD_DIM // 4
ROT_IDX = IDX_DIM // 4
PEER_TOPK = 16
NEG_BIG = -1e30

KEY_NEG_INF = -2139095041
KEY_POS_INF_P1 = 2139095041

_NT = (((1,), (1,)), ((), ()))
_TN = (((0,), (0,)), ((), ()))


def _cparams(sem):
    return pltpu.CompilerParams(dimension_semantics=sem, vmem_limit_bytes=VMEM_LIMIT)


def _ln(x):
    mu = jnp.mean(x, axis=-1, keepdims=True)
    xc = x - mu
    var = jnp.mean(xc * xc, axis=-1, keepdims=True)
    return xc * lax.rsqrt(var + LN_EPS)


def _gelu(x):
    return 0.5 * x * (1.0 + jnp.tanh(math.sqrt(2.0 / math.pi) * (x + 0.044715 * (x * x * x))))


def _ada_kernel(c_ref, w_ref, b_ref, o_ref):
    c = c_ref[...]
    s = c * jax.nn.sigmoid(c)
    o_ref[...] = jnp.dot(s, w_ref[...], preferred_element_type=F32) + b_ref[...]


def _ada(c_pad, w, b, tn=512):
    rows, d = c_pad.shape
    n6 = w.shape[1]
    return pl.pallas_call(
        _ada_kernel,
        grid=(n6 // tn,),
        in_specs=[
            pl.BlockSpec((rows, d), lambda j: (0, 0)),
            pl.BlockSpec((d, tn), lambda j: (0, j)),
            pl.BlockSpec((1, tn), lambda j: (0, j)),
        ],
        out_specs=pl.BlockSpec((rows, tn), lambda j: (0, j)),
        out_shape=jax.ShapeDtypeStruct((rows, n6), F32),
        compiler_params=_cparams(("arbitrary",)),
        name="ada",
    )(c_pad, w, b)


def _inproj_kernel(x_ref, mod_ref, w_ref, o_ref, u_ref):
    @pl.when(pl.program_id(1) == 0)
    def _():
        m = mod_ref[0]
        u = _ln(x_ref[...]) * (1.0 + m[1:2]) + m[0:1]
        u_ref[...] = u.astype(BF16)

    o_ref[...] = jnp.dot(u_ref[...], w_ref[...], preferred_element_type=F32)


def _inproj(x, mod, w, seq, tm, tn):
    n, d = x.shape
    nout = w.shape[1]
    return pl.pallas_call(
        _inproj_kernel,
        grid=(n // tm, nout // tn),
        in_specs=[
            pl.BlockSpec((tm, d), lambda i, j: (i, 0)),
            pl.BlockSpec((1, 6, d), lambda i, j: (i * tm // seq, 0, 0)),
            pl.BlockSpec((d, tn), lambda i, j: (0, j)),
        ],
        out_specs=pl.BlockSpec((tm, tn), lambda i, j: (i, j)),
        out_shape=jax.ShapeDtypeStruct((n, nout), F32),
        scratch_shapes=[pltpu.VMEM((tm, d), BF16)],
        compiler_params=_cparams(("arbitrary", "arbitrary")),
        name="inproj",
    )(x, mod, w)


def _lru_kernel(xl_ref, gl_ref, cw_ref, cb_ref, wa_ref, ba_ref, wx_ref, bx_ref, lam_ref,
                o_ref, xe_ref, a_ref, b_ref, h_ref, *, nblk):
    tc, c = xl_ref.shape

    @pl.when(pl.program_id(1) == 0)
    def _():
        xe_ref[0:SUBLANES, :] = jnp.zeros((SUBLANES, c), F32)
        h_ref[...] = jnp.zeros_like(h_ref)

    x = xl_ref[...]
    xe_ref[SUBLANES:SUBLANES + tc, :] = x
    cw = cw_ref[...]
    xc = (x * cw[3:4]
          + xe_ref[pl.ds(SUBLANES - 1, tc), :] * cw[2:3]
          + xe_ref[pl.ds(SUBLANES - 2, tc), :] * cw[1:2]
          + xe_ref[pl.ds(SUBLANES - 3, tc), :] * cw[0:1]
          + cb_ref[...])
    xe_ref[0:SUBLANES, :] = x[tc - SUBLANES:tc, :]

    lam = lam_ref[...]
    z = -lam
    softplus = jnp.maximum(z, 0.0) + jnp.log1p(jnp.exp(-jnp.abs(z)))
    for nb in range(nblk):
        sl = slice(nb * LANES, (nb + 1) * LANES)
        xb = xc[:, sl]
        xb16 = xb.astype(BF16)
        r = jax.nn.sigmoid(jnp.dot(xb16, wa_ref[nb], preferred_element_type=F32) + ba_ref[:, sl])
        i = jax.nn.sigmoid(jnp.dot(xb16, wx_ref[nb], preferred_element_type=F32) + bx_ref[:, sl])
        log_a = (-LRU_C) * r * softplus[:, sl]
        a = jnp.exp(log_a)
        one_m_a2 = -jnp.tanh(log_a) * (a * a + 1.0)
        a_ref[:, sl] = a
        b_ref[:, sl] = jnp.sqrt(one_m_a2) * (i * xb)

    row = lax.broadcasted_iota(I32, (SUBLANES, c), 0)

    def body(g, h):
        r0 = pl.multiple_of(g * SUBLANES, SUBLANES)
        a = a_ref[pl.ds(r0, SUBLANES), :]
        b = b_ref[pl.ds(r0, SUBLANES), :]
        for d in (1, 2, 4):
            keep = row >= d
            a_s = pltpu.roll(a, d, 0)
            b_s = pltpu.roll(b, d, 0)
            b = jnp.where(keep, a * b_s + b, b)
            a = jnp.where(keep, a * a_s, a)
        hh = a * h + b
        o_ref[pl.ds(r0, SUBLANES), :] = (hh * _gelu(gl_ref[pl.ds(r0, SUBLANES), :])).astype(o_ref.dtype)
        return hh[SUBLANES - 1:SUBLANES, :]

    h_ref[0:1, :] = lax.fori_loop(0, tc // SUBLANES, body, h_ref[0:1, :])


def _lru(proj, cw, cb, wa, ba, wx, bx, lam, batch, seq, tc):
    n = proj.shape[0]
    c = cw.shape[1]
    nblk = wa.shape[0]
    nt = seq // tc
    full = lambda shape: pl.BlockSpec(shape, lambda b, t: (0,) * len(shape))
    return pl.pallas_call(
        functools.partial(_lru_kernel, nblk=nblk),
        grid=(batch, nt),
        in_specs=[
            pl.BlockSpec((tc, c), lambda b, t: (b * nt + t, 0)),
            pl.BlockSpec((tc, c), lambda b, t: (b * nt + t, 1)),
            full((CONV_W, c)), full((1, c)),
            full((nblk, LANES, LANES)), full((1, c)),
            full((nblk, LANES, LANES)), full((1, c)),
            full((1, c)),
        ],
        out_specs=pl.BlockSpec((tc, c), lambda b, t: (b * nt + t, 0)),
        out_shape=jax.ShapeDtypeStruct((n, c), BF16),
        scratch_shapes=[
            pltpu.VMEM((tc + SUBLANES, c), F32),
            pltpu.VMEM((tc, c), F32),
            pltpu.VMEM((tc, c), F32),
            pltpu.VMEM((SUBLANES, c), F32),
        ],
        compiler_params=_cparams(("arbitrary", "arbitrary")),
        name="lru",
    )(proj, proj, cw, cb, wa, ba, wx, bx, lam)


def _rope_tables(ang, lane_in_head, half):
    cos = jnp.cos(ang)
    sin = jnp.sin(ang)
    c = jnp.where(lane_in_head < 2 * half, cos, 1.0)
    s1 = jnp.where(lane_in_head < half, -sin, 0.0)
    s2 = jnp.where((lane_in_head >= half) & (lane_in_head < 2 * half), sin, 0.0)
    return c, s1, s2


def _rope(x, tabs, half):
    c, s1, s2 = tabs
    return x * c + pltpu.roll(x, LANES - half, 1) * s1 + pltpu.roll(x, half, 1) * s2


def _dsaprep_kernel(q_ref, k_ref, v_ref, qi_ref, kw_ref, pos_ref, g_ref, fa_ref, fi_ref,
                    qo_ref, ko_ref, vo_ref, qio_ref, kdo_ref, wo_ref, *, nh, npair, wscale):
    tp = q_ref.shape[0]
    pos = pos_ref[...].astype(F32)
    lane = lax.broadcasted_iota(I32, (tp, LANES), 1)
    tab_a = _rope_tables(pos * fa_ref[...], lane, ROT_ATT // 2)
    tab_i = _rope_tables(pos * fi_ref[...], lane % IDX_DIM, ROT_IDX // 2)
    for h in range(nh):
        sl = slice(h * LANES, (h + 1) * LANES)
        qo_ref[:, sl] = _rope(q_ref[:, sl], tab_a, ROT_ATT // 2).astype(BF16)
        ko_ref[:, sl] = _rope(k_ref[:, sl], tab_a, ROT_ATT // 2).astype(BF16)
    vo_ref[...] = v_ref[...].astype(BF16)
    first = lane < IDX_DIM
    for p in range(npair):
        r = _rope(qi_ref[:, p * LANES:(p + 1) * LANES], tab_i, ROT_IDX // 2)
        qio_ref[:, (2 * p) * LANES:(2 * p + 1) * LANES] = jnp.where(first, r, 0.0).astype(BF16)
        qio_ref[:, (2 * p + 1) * LANES:(2 * p + 2) * LANES] = jnp.where(first, 0.0, r).astype(BF16)
    kw = kw_ref[...]
    mu = jnp.sum(jnp.where(first, kw, 0.0), axis=1, keepdims=True) * (1.0 / IDX_DIM)
    kc = jnp.where(first, kw - mu, 0.0)
    var = jnp.sum(kc * kc, axis=1, keepdims=True) * (1.0 / IDX_DIM)
    kn = kc * lax.rsqrt(var + LN_EPS) * g_ref[...]
    kr = _rope(kn, tab_i, ROT_IDX // 2)
    kr = jnp.where(first, kr, 0.0)
    kdo_ref[...] = (kr + pltpu.roll(kr, IDX_DIM, 1)).astype(BF16)
    wo_ref[...] = kw * wscale


def _dsaprep(proj, pos, g_pad, fa, fi, d_lru, d_att, ni, tp):
    n = proj.shape[0]
    nh = d_att // HEAD_DIM
    npair = ni * IDX_DIM // LANES
    assert d_lru % d_att == 0 or d_att % d_lru == 0
    q0 = 2 * d_lru // d_att
    kw_blk = (2 * d_lru + 3 * d_att + ni * IDX_DIM) // LANES
    assert (ni * IDX_DIM) % d_att == 0 or ni * IDX_DIM == d_att
    wide = lambda j: pl.BlockSpec((tp, d_att), lambda i: (i, j))
    row1 = lambda: pl.BlockSpec((1, LANES), lambda i: (0, 0))
    out_w = lambda: pl.BlockSpec((tp, d_att), lambda i: (i, 0))
    out_n = lambda: pl.BlockSpec((tp, LANES), lambda i: (i, 0))
    wscale = float(ni) ** -0.5 * float(IDX_DIM) ** -0.5
    return pl.pallas_call(
        functools.partial(_dsaprep_kernel, nh=nh, npair=npair, wscale=wscale),
        grid=(n // tp,),
        in_specs=[
            wide(q0), wide(q0 + 1), wide(q0 + 2),
            pl.BlockSpec((tp, ni * IDX_DIM), lambda i: (i, (2 * d_lru + 3 * d_att) // (ni * IDX_DIM))),
            pl.BlockSpec((tp, LANES), lambda i: (i, kw_blk)),
            pl.BlockSpec((tp, 1), lambda i: (i, 0)),
            row1(), row1(), row1(),
        ],
        out_specs=[out_w(), out_w(), out_w(),
                   pl.BlockSpec((tp, ni * LANES), lambda i: (i, 0)), out_n(), out_n()],
        out_shape=[
            jax.ShapeDtypeStruct((n, d_att), BF16),
            jax.ShapeDtypeStruct((n, d_att), BF16),
            jax.ShapeDtypeStruct((n, d_att), BF16),
            jax.ShapeDtypeStruct((n, ni * LANES), BF16),
            jax.ShapeDtypeStruct((n, LANES), BF16),
            jax.ShapeDtypeStruct((n, LANES), F32),
        ],
        compiler_params=_cparams(("arbitrary",)),
        name="dsaprep",
    )(proj, proj, proj, proj, proj, pos, g_pad, fa, fi)


def _f32_key(x):
    bits = lax.bitcast_convert_type(x + 0.0, I32)
    return jnp.where(bits < 0, bits ^ jnp.int32(0x7FFFFFFF), bits)


def _dsa_kernel(q_ref, qi_ref, w_ref, k_ref, v_ref, kd_ref, o_ref, keys_ref, bias_ref, wb_ref,
                *, tq, tk, nh, ni, topk, att_scale):
    qb = pl.program_id(1)
    ntile = ((qb + 1) * tq + tk - 1) // tk
    nsub = tk // LANES

    w = w_ref[...]
    for h in range(ni):
        wb_ref[h] = jnp.broadcast_to(w[:, IDX_DIM + h:IDX_DIM + h + 1], (tq, LANES))
    row = lax.broadcasted_iota(I32, (tq, tk), 0)
    lane = lax.broadcasted_iota(I32, (tq, tk), 1)
    bound = (((qb * tq + row) >> CHUNK_SHIFT) + 1) << CHUNK_SHIFT

    def score_tile(j, carry):
        k0 = pl.multiple_of(j * tk, tk)
        kd = kd_ref[pl.ds(k0, tk), :]
        acc = [jnp.zeros((tq, LANES), F32) for _ in range(nsub)]
        for h in range(ni):
            qh = qi_ref[:, h * LANES:(h + 1) * LANES]
            logit = lax.dot_general(qh, kd, _NT, preferred_element_type=F32)
            wh = wb_ref[h]
            for s in range(nsub):
                acc[s] = acc[s] + wh * jnp.maximum(logit[:, s * LANES:(s + 1) * LANES], 0.0)
        score = jnp.concatenate(acc, axis=1)
        adm = (k0 + lane) < bound
        keys_ref[j] = _f32_key(jnp.where(adm, score, -jnp.inf))
        return carry

    lax.fori_loop(0, ntile, score_tile, 0)

    def count_ge(mid):
        def body(j, acc):
            ge = jnp.where(keys_ref[j] >= mid, 1.0, 0.0)
            for s in range(nsub):
                acc = acc + ge[:, s * LANES:(s + 1) * LANES]
            return acc
        acc = lax.fori_loop(0, ntile, body, jnp.zeros((tq, LANES), F32))
        return jnp.sum(acc, axis=1, keepdims=True)

    def bisect(_, carry):
        lo, hi = carry
        mid = (lo & hi) + ((lo ^ hi) >> 1)
        ge = count_ge(mid) >= float(topk)
        return jnp.where(ge, mid, lo), jnp.where(ge, hi, mid)

    lo0 = jnp.full((tq, 1), KEY_NEG_INF, I32)
    hi0 = jnp.full((tq, 1), KEY_POS_INF_P1, I32)
    thr, _ = lax.fori_loop(0, 32, bisect, (lo0, hi0))

    def bias_tile(j, carry):
        k0 = pl.multiple_of(j * tk, tk)
        sel = (keys_ref[j] >= thr) & ((k0 + lane) < bound)
        bias_ref[j] = jnp.where(sel, 0.0, NEG_BIG)
        return carry

    lax.fori_loop(0, ntile, bias_tile, 0)

    for h in range(nh):
        sl = slice(h * HEAD_DIM, (h + 1) * HEAD_DIM)
        qh = q_ref[:, sl]

        def att_tile(j, carry, sl=sl, qh=qh):
            m, l, acc = carry
            k0 = pl.multiple_of(j * tk, tk)
            kt = k_ref[pl.ds(k0, tk), sl]
            vt = v_ref[pl.ds(k0, tk), sl]
            s = lax.dot_general(qh, kt, _NT, preferred_element_type=F32) * att_scale + bias_ref[j]
            m_new = jnp.maximum(m, jnp.max(s, axis=1, keepdims=True))
            alpha = jnp.exp(m - m_new)
            p = jnp.exp(s - m_new)
            l = alpha * l + jnp.sum(p, axis=1, keepdims=True)
            acc = alpha * acc + jnp.dot(p.astype(BF16), vt, preferred_element_type=F32)
            return m_new, l, acc

        m0 = jnp.full((tq, 1), NEG_BIG, F32)
        l0 = jnp.zeros((tq, 1), F32)
        a0 = jnp.zeros((tq, HEAD_DIM), F32)
        _, l, acc = lax.fori_loop(0, ntile, att_tile, (m0, l0, a0))
        o_ref[:, sl] = (acc / l).astype(o_ref.dtype)


def _dsa(q, k, v, qi, kd, w, batch, seq, tq, tk):
    n, d_att = q.shape
    nh = d_att // HEAD_DIM
    ni = qi.shape[1] // LANES
    topk = min(TOPK_MAX, seq // 4)
    nq = seq // tq
    assert tq == Q_BLOCK and seq % tk == 0 and tk % CHUNK == 0 and tk >= topk
    qspec = lambda width: pl.BlockSpec((tq, width), lambda b, i: (b * nq + i, 0))
    kspec = lambda width: pl.BlockSpec((seq, width), lambda b, i: (b, 0))
    return pl.pallas_call(
        functools.partial(_dsa_kernel, tq=tq, tk=tk, nh=nh, ni=ni, topk=topk,
                          att_scale=float(HEAD_DIM) ** -0.5),
        grid=(batch, nq),
        in_specs=[qspec(d_att), qspec(ni * LANES), qspec(LANES),
                  kspec(d_att), kspec(d_att), kspec(LANES)],
        out_specs=qspec(d_att),
        out_shape=jax.ShapeDtypeStruct((n, d_att), BF16),
        scratch_shapes=[
            pltpu.VMEM((seq // tk, tq, tk), I32),
            pltpu.VMEM((seq // tk, tq, tk), F32),
            pltpu.VMEM((ni, tq, LANES), F32),
        ],
        compiler_params=_cparams(("arbitrary", "arbitrary")),
        name="dsa",
    )(q, qi, w, k, v, kd)


def _outproj_kernel(yl_ref, ya_ref, x_ref, mod_ref, wl_ref, wa_ref, g_ref, b_ref,
                    x1_ref, u2_ref, *, alpha):
    m = mod_ref[0]
    mix = (jnp.dot(yl_ref[...], wl_ref[...], preferred_element_type=F32)
           + jnp.dot(ya_ref[...], wa_ref[...], preferred_element_type=F32))
    x1 = _ln(alpha * x_ref[...] + (1.0 + m[2:3]) * mix) * g_ref[...] + b_ref[...]
    x1_ref[...] = x1
    u2_ref[...] = (_ln(x1) * (1.0 + m[4:5]) + m[3:4]).astype(BF16)


def _outproj(yl, ya, x, mod, wl, wa, g, b, seq, alpha, tm):
    n, d = x.shape
    row = lambda width: pl.BlockSpec((tm, width), lambda i: (i, 0))
    full = lambda shape: pl.BlockSpec(shape, lambda i: (0,) * len(shape))
    return pl.pallas_call(
        functools.partial(_outproj_kernel, alpha=alpha),
        grid=(n // tm,),
        in_specs=[row(yl.shape[1]), row(ya.shape[1]), row(d),
                  pl.BlockSpec((1, 6, d), lambda i: (i * tm // seq, 0, 0)),
                  full(wl.shape), full(wa.shape), full((1, d)), full((1, d))],
        out_specs=[row(d), row(d)],
        out_shape=[jax.ShapeDtypeStruct((n, d), F32), jax.ShapeDtypeStruct((n, d), BF16)],
        compiler_params=_cparams(("arbitrary",)),
        name="outproj",
    )(yl, ya, x, mod, wl, wa, g, b)


def _topk_desc(x, k):
    out = []
    for _ in range(k):
        m = jnp.max(x, axis=0, keepdims=True)
        out.append(m)
        x = jnp.where(x == m, -jnp.inf, x)
    return out


def _route_kernel(u_ref, wpq_ref, sk_ref, s1_ref, e1_ref, s2_ref, e2_ref, tau_ref,
                  v1_ref, v2_ref, cand_ref, *, heads):
    tm = u_ref.shape[0]
    nlt = tm // LANES
    qp = jnp.dot(u_ref[...], wpq_ref[...], preferred_element_type=F32).astype(BF16)
    cand_ref[...] = jnp.full(cand_ref.shape, -jnp.inf, F32)
    for h in range(heads):
        st = []
        for p in range(2):
            col = (2 * h + p) * LANES
            st.append(lax.dot_general(sk_ref[h, p], qp[:, col:col + LANES], _NT,
                                      preferred_element_type=F32))
        s1, s2 = st
        for k, m in enumerate(_topk_desc(s1, PEER_TOPK)):
            v1_ref[k:k + 1, :] = m
        for k, m in enumerate(_topk_desc(s2, PEER_TOPK)):
            v2_ref[k:k + 1, :] = m
        off = 0
        for i in range(PEER_TOPK):
            nj = PEER_TOPK // (i + 1)
            cand_ref[off:off + nj, :] = v1_ref[i:i + 1, :] + v2_ref[0:nj, :]
            off += nj
        m1 = v1_ref[0:1, :]
        m2 = v2_ref[0:1, :]
        top = _topk_desc(cand_ref[...], PEER_TOPK)
        tau = top[-1]
        z = top[0] * 0.0
        for c in top:
            z = z + jnp.exp(c - (m1 + m2))
        e1 = jnp.exp(s1 - m1) / z
        e2 = jnp.exp(s2 - m2)
        tau8 = jnp.broadcast_to(tau, (SUBLANES, tm))
        for lt in range(nlt):
            ls = slice(lt * LANES, (lt + 1) * LANES)
            s1_ref[h, lt] = s1[:, ls]
            e1_ref[h, lt] = e1[:, ls]
            s2_ref[h, lt] = s2[:, ls]
            e2_ref[h, lt] = e2[:, ls]
            tau_ref[h, lt] = tau8[:, ls]


def _route(u2, wpq, sk, tm):
    n, d = u2.shape
    heads, _, nkeys, kd = sk.shape
    assert nkeys == LANES and kd == LANES
    nt = n // LANES
    nlt = tm // LANES
    ncand = sum(PEER_TOPK // (i + 1) for i in range(PEER_TOPK))
    ncand_pad = -(-ncand // SUBLANES) * SUBLANES
    tile = lambda: pl.BlockSpec((heads, nlt, nkeys, LANES), lambda i: (0, i, 0, 0))
    shp = jax.ShapeDtypeStruct((heads, nt, nkeys, LANES), F32)
    return pl.pallas_call(
        functools.partial(_route_kernel, heads=heads),
        grid=(n // tm,),
        in_specs=[
            pl.BlockSpec((tm, d), lambda i: (i, 0)),
            pl.BlockSpec(wpq.shape, lambda i: (0, 0)),
            pl.BlockSpec(sk.shape, lambda i: (0, 0, 0, 0)),
        ],
        out_specs=[tile(), tile(), tile(), tile(),
                   pl.BlockSpec((heads, nlt, SUBLANES, LANES), lambda i: (0, i, 0, 0))],
        out_shape=[shp, shp, shp, shp,
                   jax.ShapeDtypeStruct((heads, nt, SUBLANES, LANES), F32)],
        scratch_shapes=[
            pltpu.VMEM((PEER_TOPK, tm), F32),
            pltpu.VMEM((PEER_TOPK, tm), F32),
            pltpu.VMEM((ncand_pad, tm), F32),
        ],
        compiler_params=_cparams(("arbitrary",)),
        name="route",
    )(u2, wpq, sk)


def _peer_kernel(u_ref, ut_ref, vt_ref, s1_ref, e1_ref, s2_ref, e2_ref, tau_ref,
                 o_ref, act_ref, hk_ref, *, heads, na):
    j = pl.program_id(1)
    tm = u_ref.shape[0]
    nlt = tm // LANES

    @pl.when(j == 0)
    def _():
        o_ref[...] = jnp.zeros_like(o_ref)

    act_ref[...] = lax.dot_general(ut_ref[...], u_ref[...], _NT, preferred_element_type=F32)

    def gate_rows(ai, carry):
        a = j * na + ai
        r0 = pl.multiple_of(ai * LANES, LANES)
        for lt in range(nlt):
            ls = slice(lt * LANES, (lt + 1) * LANES)
            g = jnp.zeros((LANES, LANES), F32)
            for h in range(heads):
                s1 = s1_ref[h, lt, pl.ds(a, 1), :]
                e1 = e1_ref[h, lt, pl.ds(a, 1), :]
                tau = tau_ref[h, lt, 0:1, :]
                g = g + jnp.where(s1 + s2_ref[h, lt] >= tau, e1 * e2_ref[h, lt], 0.0)
            hk_ref[pl.ds(r0, LANES), ls] = (g * _gelu(act_ref[pl.ds(r0, LANES), ls])).astype(BF16)
        return carry

    lax.fori_loop(0, na, gate_rows, 0)
    o_ref[...] += lax.dot_general(hk_ref[...], vt_ref[...], _TN, preferred_element_type=F32)


def _peer(u2, ut, vt, s1, e1, s2, e2, tau, tm, te):
    n, d = u2.shape
    ne = ut.shape[0]
    heads = s1.shape[0]
    nlt = tm // LANES
    na = te // LANES
    tile = lambda rows: pl.BlockSpec((heads, nlt, rows, LANES), lambda i, j: (0, i, 0, 0))
    return pl.pallas_call(
        functools.partial(_peer_kernel, heads=heads, na=na),
        grid=(n // tm, ne // te),
        in_specs=[
            pl.BlockSpec((tm, d), lambda i, j: (i, 0)),
            pl.BlockSpec((te, d), lambda i, j: (j, 0)),
            pl.BlockSpec((te, d), lambda i, j: (j, 0)),
            tile(LANES), tile(LANES), tile(LANES), tile(LANES), tile(SUBLANES),
        ],
        out_specs=pl.BlockSpec((tm, d), lambda i, j: (i, 0)),
        out_shape=jax.ShapeDtypeStruct((n, d), F32),
        scratch_shapes=[pltpu.VMEM((te, tm), F32), pltpu.VMEM((te, tm), BF16)],
        compiler_params=_cparams(("arbitrary", "arbitrary")),
        name="peer",
    )(u2, ut, vt, s1, e1, s2, e2, tau)


def _final_kernel(x1_ref, y_ref, mod_ref, g_ref, b_ref, o_ref, *, alpha):
    m = mod_ref[0]
    o_ref[...] = _ln(alpha * x1_ref[...] + (1.0 + m[5:6]) * y_ref[...]) * g_ref[...] + b_ref[...]


def _final(x1, y, mod, g, b, seq, alpha, tm):
    n, d = x1.shape
    row = lambda: pl.BlockSpec((tm, d), lambda i: (i, 0))
    full = lambda: pl.BlockSpec((1, d), lambda i: (0, 0))
    return pl.pallas_call(
        functools.partial(_final_kernel, alpha=alpha),
        grid=(n // tm,),
        in_specs=[row(), row(), pl.BlockSpec((1, 6, d), lambda i: (i * tm // seq, 0, 0)),
                  full(), full()],
        out_specs=row(),
        out_shape=jax.ShapeDtypeStruct((n, d), F32),
        compiler_params=_cparams(("arbitrary",)),
        name="final",
    )(x1, y, mod, g, b)


def _tile(total, want):
    t = min(total, want)
    while total % t:
        t //= 2
    return t


def _freq_row(rot, period):
    half = rot // 2
    inv = ROPE_THETA ** (-np.arange(half, dtype=np.float32) / half)
    lane = np.arange(LANES) % period
    return jnp.asarray(np.where(lane < rot, inv[lane % half], 0.0).astype(np.float32)[None, :])


def kernel(x, c, positions, w_ada, b_ada, w_in, conv_w, conv_b, w_rg_a, b_rg_a, w_rg_x, b_rg_x,
           lru_lambda, k_idx_g, w_out, ln1_g, ln1_b, w_pq, peer_sub_keys, peer_u, peer_v,
           ln2_g, ln2_b):
    batch, seq, d = x.shape
    depth = w_ada.shape[0]
    n = batch * seq
    d_lru = conv_w.shape[-1]
    d_att = w_out.shape[1] - d_lru
    d_in = w_in.shape[-1]
    ni = (d_in - 2 * d_lru - 3 * d_att - IDX_DIM) // (IDX_DIM + 1)
    assert 2 * d_lru + 3 * d_att + ni * IDX_DIM + IDX_DIM + ni == d_in
    alpha = (2.0 * depth) ** 0.25

    nout = -(-d_in // (2 * LANES)) * (2 * LANES)
    tn_in = _tile(nout, 1280) if nout % 1280 == 0 else 2 * LANES
    pos = positions.reshape(n, 1).astype(I32)
    fa = _freq_row(ROT_ATT, HEAD_DIM)
    fi = _freq_row(ROT_IDX, IDX_DIM)
    c_pad = jnp.pad(c, ((0, SUBLANES - batch), (0, 0)))

    xf = x.reshape(n, d)
    for l in range(depth):
        mod = _ada(c_pad, w_ada[l], b_ada[l][None, :])[:batch].reshape(batch, 6, d)
        w_in_p = jnp.pad(w_in[l], ((0, 0), (0, nout - d_in))).astype(BF16)
        proj = _inproj(xf, mod, w_in_p, seq, _tile(seq, 1024), tn_in)
        y_lru = _lru(proj, conv_w[l], conv_b[l][None, :], w_rg_a[l].astype(BF16), b_rg_a[l][None, :],
                     w_rg_x[l].astype(BF16), b_rg_x[l][None, :], lru_lambda[l][None, :],
                     batch, seq, _tile(seq, 256))
        g_pad = jnp.pad(k_idx_g[l], (0, LANES - IDX_DIM))[None, :]
        q, k, v, qi, kd, w = _dsaprep(proj, pos, g_pad, fa, fi, d_lru, d_att, ni, _tile(seq, 512))
        y_att = _dsa(q, k, v, qi, kd, w, batch, seq, Q_BLOCK, _tile(seq, 512))
        wo = w_out[l].astype(BF16)
        x1, u2 = _outproj(y_lru, y_att, xf, mod, wo[:d_lru], wo[d_lru:], ln1_g[l][None, :],
                          ln1_b[l][None, :], seq, alpha, _tile(seq, 256))
        s1, e1, s2, e2, tau = _route(u2, w_pq[l].astype(BF16), peer_sub_keys[l].astype(BF16),
                                     _tile(seq, 256))
        y = _peer(u2, peer_u[l].astype(BF16), peer_v[l].astype(BF16), s1, e1, s2, e2, tau,
                  _tile(seq, 512), _tile(peer_u.shape[1], 512))
        xf = _final(x1, y, mod, ln2_g[l][None, :], ln2_b[l][None, :], seq, alpha, _tile(seq, 512))
    return xf.reshape(batch, seq, d)
```

```python
import functools
import math

import jax
import jax.numpy as jnp
import numpy as np
from jax import lax
from jax.experimental import pallas as pl
from jax.experimental.pallas import tpu as pltpu

F32 = jnp.float32
BF16 = jnp.bfloat16
I32 = jnp.int32

LANES = 128
SUBLANES = 8
VMEM_LIMIT = 56 * 1024 * 1024

LN_EPS = 1e-5
CHUNK = 64
CHUNK_SHIFT = 6
assert 1 << CHUNK_SHIFT == CHUNK
CONV_W = 4
LRU_C = 8.0
HEAD_DIM = 128
IDX_DIM = 64
TOPK_MAX = 256
Q_BLOCK = 128
ROPE_THETA = 500000.0
ROT_ATT = HEAD_DIM // 4
ROT_IDX = IDX_DIM // 4
PEER_TOPK = 16
NEG_BIG = -1e30

MAX_BISECT = 300

_NT = (((1,), (1,)), ((), ()))
_TN = (((0,), (0,)), ((), ()))


def _cparams(sem):
    return pltpu.CompilerParams(dimension_semantics=sem, vmem_limit_bytes=VMEM_LIMIT)


def _ln(x):
    mu = jnp.mean(x, axis=-1, keepdims=True)
    xc = x - mu
    var = jnp.mean(xc * xc, axis=-1, keepdims=True)
    return xc * lax.rsqrt(var + LN_EPS)


def _gelu(x):
    return 0.5 * x * (1.0 + jnp.tanh(math.sqrt(2.0 / math.pi) * (x + 0.044715 * (x * x * x))))


def _ada_kernel(c_ref, w_ref, b_ref, o_ref):
    c = c_ref[...]
    s = c * jax.nn.sigmoid(c)
    o_ref[...] = jnp.dot(s, w_ref[...], preferred_element_type=F32) + b_ref[...]


def _ada(c_pad, w, b, tn=512):
    rows, d = c_pad.shape
    n6 = w.shape[1]
    return pl.pallas_call(
        _ada_kernel,
        grid=(n6 // tn,),
        in_specs=[
            pl.BlockSpec((rows, d), lambda j: (0, 0)),
            pl.BlockSpec((d, tn), lambda j: (0, j)),
            pl.BlockSpec((1, tn), lambda j: (0, j)),
        ],
        out_specs=pl.BlockSpec((rows, tn), lambda j: (0, j)),
        out_shape=jax.ShapeDtypeStruct((rows, n6), F32),
        compiler_params=_cparams(("arbitrary",)),
        name="ada",
    )(c_pad, w, b)


def _inproj_kernel(x_ref, mod_ref, w_ref, o_ref, u_ref):
    @pl.when(pl.program_id(1) == 0)
    def _():
        m = mod_ref[0]
        u = _ln(x_ref[...]) * (1.0 + m[1:2]) + m[0:1]
        u_ref[...] = u.astype(BF16)

    o_ref[...] = jnp.dot(u_ref[...], w_ref[...], preferred_element_type=F32)


def _inproj(x, mod, w, seq, tm, tn):
    n, d = x.shape
    nout = w.shape[1]
    return pl.pallas_call(
        _inproj_kernel,
        grid=(n // tm, nout // tn),
        in_specs=[
            pl.BlockSpec((tm, d), lambda i, j: (i, 0)),
            pl.BlockSpec((1, 6, d), lambda i, j: (i * tm // seq, 0, 0)),
            pl.BlockSpec((d, tn), lambda i, j: (0, j)),
        ],
        out_specs=pl.BlockSpec((tm, tn), lambda i, j: (i, j)),
        out_shape=jax.ShapeDtypeStruct((n, nout), F32),
        scratch_shapes=[pltpu.VMEM((tm, d), BF16)],
        compiler_params=_cparams(("arbitrary", "arbitrary")),
        name="inproj",
    )(x, mod, w)


def _lru_kernel(xl_ref, gl_ref, cw_ref, cb_ref, wa_ref, ba_ref, wx_ref, bx_ref, lam_ref,
                o_ref, xe_ref, a_ref, b_ref, h_ref, *, nblk):
    tc, c = xl_ref.shape

    @pl.when(pl.program_id(1) == 0)
    def _():
        xe_ref[0:SUBLANES, :] = jnp.zeros((SUBLANES, c), F32)
        h_ref[...] = jnp.zeros_like(h_ref)

    x = xl_ref[...]
    xe_ref[SUBLANES:SUBLANES + tc, :] = x
    cw = cw_ref[...]
    xc = (x * cw[3:4]
          + xe_ref[pl.ds(SUBLANES - 1, tc), :] * cw[2:3]
          + xe_ref[pl.ds(SUBLANES - 2, tc), :] * cw[1:2]
          + xe_ref[pl.ds(SUBLANES - 3, tc), :] * cw[0:1]
          + cb_ref[...])
    xe_ref[0:SUBLANES, :] = x[tc - SUBLANES:tc, :]

    lam = lam_ref[...]
    z = -lam
    softplus = jnp.maximum(z, 0.0) + jnp.log1p(jnp.exp(-jnp.abs(z)))
    for nb in range(nblk):
        sl = slice(nb * LANES, (nb + 1) * LANES)
        xb = xc[:, sl]
        xb16 = xb.astype(BF16)
        r = jax.nn.sigmoid(jnp.dot(xb16, wa_ref[nb], preferred_element_type=F32) + ba_ref[:, sl])
        i = jax.nn.sigmoid(jnp.dot(xb16, wx_ref[nb], preferred_element_type=F32) + bx_ref[:, sl])
        log_a = (-LRU_C) * r * softplus[:, sl]
        a = jnp.exp(log_a)
        one_m_a2 = -jnp.tanh(log_a) * (a * a + 1.0)
        a_ref[:, sl] = a
        b_ref[:, sl] = jnp.sqrt(one_m_a2) * (i * xb)

    row = lax.broadcasted_iota(I32, (SUBLANES, c), 0)

    def body(g, h):
        r0 = pl.multiple_of(g * SUBLANES, SUBLANES)
        a = a_ref[pl.ds(r0, SUBLANES), :]
        b = b_ref[pl.ds(r0, SUBLANES), :]
        for d in (1, 2, 4):
            keep = row >= d
            a_s = pltpu.roll(a, d, 0)
            b_s = pltpu.roll(b, d, 0)
            b = jnp.where(keep, a * b_s + b, b)
            a = jnp.where(keep, a * a_s, a)
        hh = a * h + b
        o_ref[pl.ds(r0, SUBLANES), :] = (hh * _gelu(gl_ref[pl.ds(r0, SUBLANES), :])).astype(o_ref.dtype)
        return hh[SUBLANES - 1:SUBLANES, :]

    h_ref[0:1, :] = lax.fori_loop(0, tc // SUBLANES, body, h_ref[0:1, :])


def _lru(proj, cw, cb, wa, ba, wx, bx, lam, batch, seq, tc):
    n = proj.shape[0]
    c = cw.shape[1]
    nblk = wa.shape[0]
    nt = seq // tc
    full = lambda shape: pl.BlockSpec(shape, lambda b, t: (0,) * len(shape))
    return pl.pallas_call(
        functools.partial(_lru_kernel, nblk=nblk),
        grid=(batch, nt),
        in_specs=[
            pl.BlockSpec((tc, c), lambda b, t: (b * nt + t, 0)),
            pl.BlockSpec((tc, c), lambda b, t: (b * nt + t, 1)),
            full((CONV_W, c)), full((1, c)),
            full((nblk, LANES, LANES)), full((1, c)),
            full((nblk, LANES, LANES)), full((1, c)),
            full((1, c)),
        ],
        out_specs=pl.BlockSpec((tc, c), lambda b, t: (b * nt + t, 0)),
        out_shape=jax.ShapeDtypeStruct((n, c), BF16),
        scratch_shapes=[
            pltpu.VMEM((tc + SUBLANES, c), F32),
            pltpu.VMEM((tc, c), F32),
            pltpu.VMEM((tc, c), F32),
            pltpu.VMEM((SUBLANES, c), F32),
        ],
        compiler_params=_cparams(("arbitrary", "arbitrary")),
        name="lru",
    )(proj, proj, cw, cb, wa, ba, wx, bx, lam)


def _rope_tables(ang, lane_in_head, half):
    cos = jnp.cos(ang)
    sin = jnp.sin(ang)
    c = jnp.where(lane_in_head < 2 * half, cos, 1.0)
    s1 = jnp.where(lane_in_head < half, -sin, 0.0)
    s2 = jnp.where((lane_in_head >= half) & (lane_in_head < 2 * half), sin, 0.0)
    return c, s1, s2


def _rope(x, tabs, half):
    c, s1, s2 = tabs
    return x * c + pltpu.roll(x, LANES - half, 1) * s1 + pltpu.roll(x, half, 1) * s2


def _dsaprep_kernel(q_ref, k_ref, v_ref, qi_ref, kw_ref, pos_ref, g_ref, fa_ref, fi_ref,
                    qo_ref, ko_ref, vo_ref, qio_ref, kdo_ref, wo_ref, *, nh, npair, wscale):
    tp = q_ref.shape[0]
    pos = pos_ref[...].astype(F32)
    lane = lax.broadcasted_iota(I32, (tp, LANES), 1)
    tab_a = _rope_tables(pos * fa_ref[...], lane, ROT_ATT // 2)
    tab_i = _rope_tables(pos * fi_ref[...], lane % IDX_DIM, ROT_IDX // 2)
    for h in range(nh):
        sl = slice(h * LANES, (h + 1) * LANES)
        qo_ref[:, sl] = _rope(q_ref[:, sl], tab_a, ROT_ATT // 2).astype(BF16)
        ko_ref[:, sl] = _rope(k_ref[:, sl], tab_a, ROT_ATT // 2).astype(BF16)
    vo_ref[...] = v_ref[...].astype(BF16)
    first = lane < IDX_DIM
    for p in range(npair):
        r = _rope(qi_ref[:, p * LANES:(p + 1) * LANES], tab_i, ROT_IDX // 2)
        qio_ref[:, (2 * p) * LANES:(2 * p + 1) * LANES] = jnp.where(first, r, 0.0).astype(BF16)
        qio_ref[:, (2 * p + 1) * LANES:(2 * p + 2) * LANES] = jnp.where(first, 0.0, r).astype(BF16)
    kw = kw_ref[...]
    mu = jnp.sum(jnp.where(first, kw, 0.0), axis=1, keepdims=True) * (1.0 / IDX_DIM)
    kc = jnp.where(first, kw - mu, 0.0)
    var = jnp.sum(kc * kc, axis=1, keepdims=True) * (1.0 / IDX_DIM)
    kn = kc * lax.rsqrt(var + LN_EPS) * g_ref[...]
    kr = _rope(kn, tab_i, ROT_IDX // 2)
    kr = jnp.where(first, kr, 0.0)
    kdo_ref[...] = (kr + pltpu.roll(kr, IDX_DIM, 1)).astype(BF16)
    wo_ref[...] = kw * wscale


def _dsaprep(proj, pos, g_pad, fa, fi, d_lru, d_att, ni, tp):
    n = proj.shape[0]
    nh = d_att // HEAD_DIM
    npair = ni * IDX_DIM // LANES
    assert d_lru % d_att == 0 or d_att % d_lru == 0
    q0 = 2 * d_lru // d_att
    kw_blk = (2 * d_lru + 3 * d_att + ni * IDX_DIM) // LANES
    assert (ni * IDX_DIM) % d_att == 0 or ni * IDX_DIM == d_att
    wide = lambda j: pl.BlockSpec((tp, d_att), lambda i: (i, j))
    row1 = lambda: pl.BlockSpec((1, LANES), lambda i: (0, 0))
    out_w = lambda: pl.BlockSpec((tp, d_att), lambda i: (i, 0))
    out_n = lambda: pl.BlockSpec((tp, LANES), lambda i: (i, 0))
    wscale = float(ni) ** -0.5 * float(IDX_DIM) ** -0.5
    return pl.pallas_call(
        functools.partial(_dsaprep_kernel, nh=nh, npair=npair, wscale=wscale),
        grid=(n // tp,),
        in_specs=[
            wide(q0), wide(q0 + 1), wide(q0 + 2),
            pl.BlockSpec((tp, ni * IDX_DIM), lambda i: (i, (2 * d_lru + 3 * d_att) // (ni * IDX_DIM))),
            pl.BlockSpec((tp, LANES), lambda i: (i, kw_blk)),
            pl.BlockSpec((tp, 1), lambda i: (i, 0)),
            row1(), row1(), row1(),
        ],
        out_specs=[out_w(), out_w(), out_w(),
                   pl.BlockSpec((tp, ni * LANES), lambda i: (i, 0)), out_n(), out_n()],
        out_shape=[
            jax.ShapeDtypeStruct((n, d_att), BF16),
            jax.ShapeDtypeStruct((n, d_att), BF16),
            jax.ShapeDtypeStruct((n, d_att), BF16),
            jax.ShapeDtypeStruct((n, ni * LANES), BF16),
            jax.ShapeDtypeStruct((n, LANES), BF16),
            jax.ShapeDtypeStruct((n, LANES), F32),
        ],
        compiler_params=_cparams(("arbitrary",)),
        name="dsaprep",
    )(proj, proj, proj, proj, proj, pos, g_pad, fa, fi)


def _dsa_kernel(q_ref, qi_ref, w_ref, k_ref, v_ref, kd_ref, o_ref, sc_ref, bias_ref, wb_ref,
                m_ref, l_ref, acc_ref, *, tq, tk, nh, ni, topk, att_scale):
    qb = pl.program_id(1)
    ntile = ((qb + 1) * tq + tk - 1) // tk
    nsub = tk // LANES
    topk_f = float(topk)

    w = w_ref[...]
    for h in range(ni):
        wb_ref[h] = jnp.broadcast_to(w[:, IDX_DIM + h:IDX_DIM + h + 1], (tq, LANES))
    row = lax.broadcasted_iota(I32, (tq, tk), 0)
    lane = lax.broadcasted_iota(I32, (tq, tk), 1)
    bound = (((qb * tq + row) >> CHUNK_SHIFT) + 1) << CHUNK_SHIFT

    n_adm = bound[:, 0:1].astype(F32)

    def score_tile(j, carry):
        mx, mn = carry
        k0 = pl.multiple_of(j * tk, tk)
        kd = kd_ref[pl.ds(k0, tk), :]
        acc = [jnp.zeros((tq, LANES), F32) for _ in range(nsub)]
        for h in range(ni):
            qh = qi_ref[:, h * LANES:(h + 1) * LANES]
            logit = lax.dot_general(qh, kd, _NT, preferred_element_type=F32)
            wh = wb_ref[h]
            for s in range(nsub):
                acc[s] = acc[s] + wh * jnp.maximum(logit[:, s * LANES:(s + 1) * LANES], 0.0)
        score = jnp.concatenate(acc, axis=1)
        adm = (k0 + lane) < bound
        sc_ref[j] = jnp.where(adm, score, -jnp.inf)
        mx = jnp.maximum(mx, jnp.max(jnp.where(adm, score, -jnp.inf), axis=1, keepdims=True))
        mn = jnp.minimum(mn, jnp.min(jnp.where(adm, score, jnp.inf), axis=1, keepdims=True))
        return mx, mn

    mx, mn = lax.fori_loop(0, ntile, score_tile,
                           (jnp.full((tq, 1), -jnp.inf, F32), jnp.full((tq, 1), jnp.inf, F32)))

    def count_ge(mid):
        def body(j, acc):
            ge = jnp.where(sc_ref[j] >= mid, 1.0, 0.0)
            for s in range(nsub):
                acc = acc + ge[:, s * LANES:(s + 1) * LANES]
            return acc
        acc = lax.fori_loop(0, ntile, body, jnp.zeros((tq, LANES), F32))
        return jnp.sum(acc, axis=1, keepdims=True)

    small = n_adm <= topk_f

    def pending(resolved):
        return jnp.max(jnp.where(resolved, 0.0, 1.0))

    def bisect_cond(c):
        return (c[0] < MAX_BISECT) & (c[4] > 0.0)

    def bisect(c):
        it, lo, hi, cnt_lo, _ = c
        mid = 0.5 * lo + 0.5 * hi
        cnt = count_ge(mid)
        ge = cnt >= topk_f
        adjacent = (mid <= lo) | (mid >= hi)
        lo = jnp.where(ge, mid, lo)
        cnt_lo = jnp.where(ge, cnt, cnt_lo)
        hi = jnp.where(ge, hi, mid)
        return it + 1, lo, hi, cnt_lo, pending((cnt_lo == topk_f) | small | adjacent)

    _, thr, _, _, _ = lax.while_loop(
        bisect_cond, bisect, (jnp.int32(0), mn, mx, n_adm, pending((n_adm == topk_f) | small)))
    thr = jnp.where(small, -jnp.inf, thr)

    def bias_tile(j, carry):
        k0 = pl.multiple_of(j * tk, tk)
        sel = (sc_ref[j] >= thr) & ((k0 + lane) < bound)
        bias_ref[j] = jnp.where(sel, 0.0, NEG_BIG)
        return carry

    lax.fori_loop(0, ntile, bias_tile, 0)

    for h in range(nh):
        m_ref[h] = jnp.full((tq, LANES), NEG_BIG, F32)
        l_ref[h] = jnp.zeros((tq, LANES), F32)
        acc_ref[h] = jnp.zeros((tq, HEAD_DIM), F32)

    def att_tile(j, carry):
        k0 = pl.multiple_of(j * tk, tk)
        bias = bias_ref[j]
        heads_sl = [slice(h * HEAD_DIM, (h + 1) * HEAD_DIM) for h in range(nh)]
        qk = [lax.dot_general(q_ref[:, sl], k_ref[pl.ds(k0, tk), sl], _NT, preferred_element_type=F32)
              for sl in heads_sl]
        for h, sl in enumerate(heads_sl):
            vt = v_ref[pl.ds(k0, tk), sl]
            s = qk[h] * att_scale + bias
            m_old = m_ref[h]
            m_new = jnp.maximum(m_old, jnp.max(s, axis=1, keepdims=True))
            alpha = jnp.exp(m_old - m_new)
            p = jnp.exp(s - pltpu.repeat(m_new, nsub, axis=1))
            l_ref[h] = alpha * l_ref[h] + jnp.sum(p, axis=1, keepdims=True)
            acc_ref[h] = alpha * acc_ref[h] + jnp.dot(p.astype(BF16), vt, preferred_element_type=F32)
            m_ref[h] = m_new
        return carry

    lax.fori_loop(0, ntile, att_tile, 0)
    for h in range(nh):
        sl = slice(h * HEAD_DIM, (h + 1) * HEAD_DIM)
        o_ref[:, sl] = (acc_ref[h] / l_ref[h]).astype(o_ref.dtype)


def _dsa(q, k, v, qi, kd, w, batch, seq, tq, tk):
    n, d_att = q.shape
    nh = d_att // HEAD_DIM
    ni = qi.shape[1] // LANES
    topk = min(TOPK_MAX, seq // 4)
    nq = seq // tq
    assert tq == Q_BLOCK and seq % tk == 0 and tk % CHUNK == 0 and tk >= topk
    qspec = lambda width: pl.BlockSpec((tq, width), lambda b, i: (b * nq + i, 0))
    kspec = lambda width: pl.BlockSpec((seq, width), lambda b, i: (b, 0))
    return pl.pallas_call(
        functools.partial(_dsa_kernel, tq=tq, tk=tk, nh=nh, ni=ni, topk=topk,
                          att_scale=float(HEAD_DIM) ** -0.5),
        grid=(batch, nq),
        in_specs=[qspec(d_att), qspec(ni * LANES), qspec(LANES),
                  kspec(d_att), kspec(d_att), kspec(LANES)],
        out_specs=qspec(d_att),
        out_shape=jax.ShapeDtypeStruct((n, d_att), BF16),
        scratch_shapes=[
            pltpu.VMEM((seq // tk, tq, tk), F32),
            pltpu.VMEM((seq // tk, tq, tk), F32),
            pltpu.VMEM((ni, tq, LANES), F32),
            pltpu.VMEM((nh, tq, LANES), F32),
            pltpu.VMEM((nh, tq, LANES), F32),
            pltpu.VMEM((nh, tq, HEAD_DIM), F32),
        ],
        compiler_params=_cparams(("arbitrary", "arbitrary")),
        name="dsa",
    )(q, qi, w, k, v, kd)


def _outproj_kernel(yl_ref, ya_ref, x_ref, mod_ref, wl_ref, wa_ref, g_ref, b_ref,
                    x1_ref, u2_ref, *, alpha):
    m = mod_ref[0]
    mix = (jnp.dot(yl_ref[...], wl_ref[...], preferred_element_type=F32)
           + jnp.dot(ya_ref[...], wa_ref[...], preferred_element_type=F32))
    x1 = _ln(alpha * x_ref[...] + (1.0 + m[2:3]) * mix) * g_ref[...] + b_ref[...]
    x1_ref[...] = x1
    u2_ref[...] = pltpu.bitcast((_ln(x1) * (1.0 + m[4:5]) + m[3:4]).astype(BF16), I32)


def _outproj(yl, ya, x, mod, wl, wa, g, b, seq, alpha, tm):
    n, d = x.shape
    row = lambda width: pl.BlockSpec((tm, width), lambda i: (i, 0))
    full = lambda shape: pl.BlockSpec(shape, lambda i: (0,) * len(shape))
    return pl.pallas_call(
        functools.partial(_outproj_kernel, alpha=alpha),
        grid=(n // tm,),
        in_specs=[row(yl.shape[1]), row(ya.shape[1]), row(d),
                  pl.BlockSpec((1, 6, d), lambda i: (i * tm // seq, 0, 0)),
                  full(wl.shape), full(wa.shape), full((1, d)), full((1, d))],
        out_specs=[row(d), pl.BlockSpec((tm // 2, d), lambda i: (i, 0))],
        out_shape=[jax.ShapeDtypeStruct((n, d), F32), jax.ShapeDtypeStruct((n // 2, d), I32)],
        compiler_params=_cparams(("arbitrary",)),
        name="outproj",
    )(yl, ya, x, mod, wl, wa, g, b)


def _topk_desc(x, k):
    out = []
    for _ in range(k):
        m = jnp.max(x, axis=0, keepdims=True)
        out.append(m)
        x = jnp.where(x == m, -jnp.inf, x)
    return out


def _route_kernel(u_ref, wpq_ref, sk_ref, s1_ref, e1_ref, s2_ref, e2_ref, tau_ref,
                  v1_ref, v2_ref, cand_ref, *, heads):
    tm = 2 * u_ref.shape[0]
    nlt = tm // LANES
    qp = jnp.dot(pltpu.bitcast(u_ref[...], BF16), wpq_ref[...], preferred_element_type=F32).astype(BF16)
    cand_ref[...] = jnp.full(cand_ref.shape, -jnp.inf, F32)
    for h in range(heads):
        st = []
        for p in range(2):
            col = (2 * h + p) * LANES
            st.append(lax.dot_general(sk_ref[h, p], qp[:, col:col + LANES], _NT,
                                      preferred_element_type=F32))
        s1, s2 = st
        for k, m in enumerate(_topk_desc(s1, PEER_TOPK)):
            v1_ref[k:k + 1, :] = m
        for k, m in enumerate(_topk_desc(s2, PEER_TOPK)):
            v2_ref[k:k + 1, :] = m
        off = 0
        for i in range(PEER_TOPK):
            nj = PEER_TOPK // (i + 1)
            cand_ref[off:off + nj, :] = v1_ref[i:i + 1, :] + v2_ref[0:nj, :]
            off += nj
        m1 = v1_ref[0:1, :]
        m2 = v2_ref[0:1, :]
        top = _topk_desc(cand_ref[...], PEER_TOPK)
        tau = top[-1]
        z = top[0] * 0.0
        for c in top:
            z = z + jnp.exp(c - (m1 + m2))
        e1 = jnp.exp(s1 - m1) / z
        e2 = jnp.exp(s2 - m2)
        tau8 = jnp.broadcast_to(tau, (SUBLANES, tm))
        for lt in range(nlt):
            ls = slice(lt * LANES, (lt + 1) * LANES)
            s1_ref[h, lt] = s1[:, ls]
            e1_ref[h, lt] = e1[:, ls]
            s2_ref[h, lt] = s2[:, ls]
            e2_ref[h, lt] = e2[:, ls]
            tau_ref[h, lt] = tau8[:, ls]


def _route(u2, wpq, sk, tm):
    n, d = 2 * u2.shape[0], u2.shape[1]
    heads, _, nkeys, kd = sk.shape
    assert nkeys == LANES and kd == LANES
    nt = n // LANES
    nlt = tm // LANES
    ncand = sum(PEER_TOPK // (i + 1) for i in range(PEER_TOPK))
    ncand_pad = -(-ncand // SUBLANES) * SUBLANES
    tile = lambda: pl.BlockSpec((heads, nlt, nkeys, LANES), lambda i: (0, i, 0, 0))
    shp = jax.ShapeDtypeStruct((heads, nt, nkeys, LANES), F32)
    return pl.pallas_call(
        functools.partial(_route_kernel, heads=heads),
        grid=(n // tm,),
        in_specs=[
            pl.BlockSpec((tm // 2, d), lambda i: (i, 0)),
            pl.BlockSpec(wpq.shape, lambda i: (0, 0)),
            pl.BlockSpec(sk.shape, lambda i: (0, 0, 0, 0)),
        ],
        out_specs=[tile(), tile(), tile(), tile(),
                   pl.BlockSpec((heads, nlt, SUBLANES, LANES), lambda i: (0, i, 0, 0))],
        out_shape=[shp, shp, shp, shp,
                   jax.ShapeDtypeStruct((heads, nt, SUBLANES, LANES), F32)],
        scratch_shapes=[
            pltpu.VMEM((PEER_TOPK, tm), F32),
            pltpu.VMEM((PEER_TOPK, tm), F32),
            pltpu.VMEM((ncand_pad, tm), F32),
        ],
        compiler_params=_cparams(("arbitrary",)),
        name="route",
    )(u2, wpq, sk)


def _peer_kernel(u_ref, ut_ref, vt_ref, s1_ref, e1_ref, s2_ref, e2_ref, tau_ref,
                 o_ref, *scratch, heads, na, chain):
    tm = 2 * u_ref.shape[0]
    nlt = tm // LANES
    nchain = na // chain
    half = chain * LANES // 2
    act_refs, hk_refs = scratch[:nchain], scratch[nchain:]
    u = pltpu.bitcast(u_ref[...], BF16)

    @pl.when(pl.program_id(1) == 0)
    def _():
        o_ref[...] = jnp.zeros_like(o_ref)

    def act_chain(c):
        u_rows = pltpu.bitcast(ut_ref[c * half:(c + 1) * half, :], BF16)
        act_refs[c][...] = lax.dot_general(u_rows, u, _NT, preferred_element_type=F32)

    total = None
    act_chain(0)
    for c in range(nchain):
        if c + 1 < nchain:
            act_chain(c + 1)
        for al in range(chain):
            a = c * chain + al
            ar = slice(al * LANES, (al + 1) * LANES)
            for lt in range(nlt):
                ls = slice(lt * LANES, (lt + 1) * LANES)
                g = jnp.zeros((LANES, LANES), F32)
                for h in range(heads):
                    s1 = s1_ref[h, lt, a:a + 1, :]
                    e1 = e1_ref[h, lt, a:a + 1, :]
                    tau = tau_ref[h, lt, 0:1, :]
                    g = g + jnp.where(s1 + s2_ref[h, lt] >= tau, e1 * e2_ref[h, lt], 0.0)
                hk_refs[c][ar, ls] = (g * _gelu(act_refs[c][ar, ls])).astype(BF16)
        v_rows = pltpu.bitcast(vt_ref[c * half:(c + 1) * half, :], BF16)
        part = lax.dot_general(hk_refs[c][...], v_rows, _TN, preferred_element_type=F32)
        total = part if total is None else total + part
    o_ref[...] += total


def _pack_row_pairs(w):
    r, c = w.shape
    return lax.bitcast_convert_type(w.reshape(r // 2, 2, c).swapaxes(1, 2), I32)


def _peer(u2, ut, vt, s1, e1, s2, e2, tau, tm, te):
    n, d = 2 * u2.shape[0], u2.shape[1]
    ne = 2 * vt.shape[0]
    heads = s1.shape[0]
    nlt = tm // LANES
    na = te // LANES
    assert na == SUBLANES
    tile = lambda rows: pl.BlockSpec((heads, nlt, rows, LANES), lambda i, j: (0, i, 0, 0))
    arow = lambda: pl.BlockSpec((heads, nlt, na, LANES), lambda i, j: (0, i, j, 0))
    chain = 2
    nchain = na // chain
    return pl.pallas_call(
        functools.partial(_peer_kernel, heads=heads, na=na, chain=chain),
        grid=(n // tm, ne // te),
        in_specs=[
            pl.BlockSpec((tm // 2, d), lambda i, j: (i, 0)),
            pl.BlockSpec((te // 2, d), lambda i, j: (j, 0)),
            pl.BlockSpec((te // 2, d), lambda i, j: (j, 0)),
            arow(), arow(), tile(LANES), tile(LANES), tile(SUBLANES),
        ],
        out_specs=pl.BlockSpec((tm, d), lambda i, j: (i, 0)),
        out_shape=jax.ShapeDtypeStruct((n, d), F32),
        scratch_shapes=([pltpu.VMEM((chain * LANES, tm), F32) for _ in range(nchain)]
                        + [pltpu.VMEM((chain * LANES, tm), BF16) for _ in range(nchain)]),
        compiler_params=_cparams(("arbitrary", "arbitrary")),
        name="peer",
    )(u2, ut, vt, s1, e1, s2, e2, tau)


def _final_kernel(x1_ref, y_ref, mod_ref, g_ref, b_ref, o_ref, *, alpha):
    m = mod_ref[0]
    o_ref[...] = _ln(alpha * x1_ref[...] + (1.0 + m[5:6]) * y_ref[...]) * g_ref[...] + b_ref[...]


def _final(x1, y, mod, g, b, seq, alpha, tm):
    n, d = x1.shape
    row = lambda: pl.BlockSpec((tm, d), lambda i: (i, 0))
    full = lambda: pl.BlockSpec((1, d), lambda i: (0, 0))
    return pl.pallas_call(
        functools.partial(_final_kernel, alpha=alpha),
        grid=(n // tm,),
        in_specs=[row(), row(), pl.BlockSpec((1, 6, d), lambda i: (i * tm // seq, 0, 0)),
                  full(), full()],
        out_specs=row(),
        out_shape=jax.ShapeDtypeStruct((n, d), F32),
        compiler_params=_cparams(("arbitrary",)),
        name="final",
    )(x1, y, mod, g, b)


def _tile(total, want):
    t = min(total, want)
    while total % t:
        t //= 2
    return t


def _freq_row(rot, period):
    half = rot // 2
    inv = ROPE_THETA ** (-np.arange(half, dtype=np.float32) / half)
    lane = np.arange(LANES) % period
    return jnp.asarray(np.where(lane < rot, inv[lane % half], 0.0).astype(np.float32)[None, :])


def kernel(x, c, positions, w_ada, b_ada, w_in, conv_w, conv_b, w_rg_a, b_rg_a, w_rg_x, b_rg_x,
           lru_lambda, k_idx_g, w_out, ln1_g, ln1_b, w_pq, peer_sub_keys, peer_u, peer_v,
           ln2_g, ln2_b):
    batch, seq, d = x.shape
    depth = w_ada.shape[0]
    n = batch * seq
    d_lru = conv_w.shape[-1]
    d_att = w_out.shape[1] - d_lru
    d_in = w_in.shape[-1]
    ni = (d_in - 2 * d_lru - 3 * d_att - IDX_DIM) // (IDX_DIM + 1)
    assert 2 * d_lru + 3 * d_att + ni * IDX_DIM + IDX_DIM + ni == d_in
    alpha = (2.0 * depth) ** 0.25

    nout = -(-d_in // (2 * LANES)) * (2 * LANES)
    tn_in = _tile(nout, 1280) if nout % 1280 == 0 else 2 * LANES
    pos = positions.reshape(n, 1).astype(I32)
    fa = _freq_row(ROT_ATT, HEAD_DIM)
    fi = _freq_row(ROT_IDX, IDX_DIM)
    c_pad = jnp.pad(c, ((0, SUBLANES - batch), (0, 0)))

    xf = x.reshape(n, d)
    for l in range(depth):
        mod = _ada(c_pad, w_ada[l], b_ada[l][None, :])[:batch].reshape(batch, 6, d)
        w_in_p = jnp.pad(w_in[l], ((0, 0), (0, nout - d_in))).astype(BF16)
        proj = _inproj(xf, mod, w_in_p, seq, _tile(seq, 1024), tn_in)
        y_lru = _lru(proj, conv_w[l], conv_b[l][None, :], w_rg_a[l].astype(BF16), b_rg_a[l][None, :],
                     w_rg_x[l].astype(BF16), b_rg_x[l][None, :], lru_lambda[l][None, :],
                     batch, seq, _tile(seq, 256))
        g_pad = jnp.pad(k_idx_g[l], (0, LANES - IDX_DIM))[None, :]
        q, k, v, qi, kd, w = _dsaprep(proj, pos, g_pad, fa, fi, d_lru, d_att, ni, _tile(seq, 512))
        y_att = _dsa(q, k, v, qi, kd, w, batch, seq, Q_BLOCK, _tile(seq, 512))
        wo = w_out[l].astype(BF16)
        x1, u2 = _outproj(y_lru, y_att, xf, mod, wo[:d_lru], wo[d_lru:], ln1_g[l][None, :],
                          ln1_b[l][None, :], seq, alpha, _tile(seq, 256))
        s1, e1, s2, e2, tau = _route(u2, w_pq[l].astype(BF16), peer_sub_keys[l].astype(BF16),
                                     _tile(seq, 256))
        y = _peer(u2, _pack_row_pairs(peer_u[l].astype(BF16)), _pack_row_pairs(peer_v[l].astype(BF16)),
                  s1, e1, s2, e2, tau,
                  _tile(seq, 512), SUBLANES * LANES)
        xf = _final(x1, y, mod, ln2_g[l][None, :], ln2_b[l][None, :], seq, alpha, _tile(seq, 512))
    return xf.reshape(batch, seq, d)
```

```python
import functools
import math

import jax
import jax.numpy as jnp
import numpy as np
from jax import lax
from jax.experimental import pallas as pl
from jax.experimental.pallas import tpu as pltpu

F32 = jnp.float32
BF16 = jnp.bfloat16
I32 = jnp.int32

LANES = 128
SUBLANES = 8
VMEM_LIMIT = 56 * 1024 * 1024

LN_EPS = 1e-5
CHUNK = 64
CHUNK_SHIFT = 6
assert 1 << CHUNK_SHIFT == CHUNK
CONV_W = 4
LRU_C = 8.0
HEAD_DIM = 128
IDX_DIM = 64
TOPK_MAX = 256
Q_BLOCK = 128
ROPE_THETA = 500000.0
ROT_ATT = HEAD_DIM // 4
ROT_IDX = IDX_DIM // 4
PEER_TOPK = 16
NEG_BIG = -1e30

MAX_BISECT = 300

_NT = (((1,), (1,)), ((), ()))
_TN = (((0,), (0,)), ((), ()))


def _cparams(sem):
    return pltpu.CompilerParams(dimension_semantics=sem, vmem_limit_bytes=VMEM_LIMIT)


def _ln(x):
    mu = jnp.mean(x, axis=-1, keepdims=True)
    xc = x - mu
    var = jnp.mean(xc * xc, axis=-1, keepdims=True)
    return xc * lax.rsqrt(var + LN_EPS)


def _gelu(x):
    return 0.5 * x * (1.0 + jnp.tanh(math.sqrt(2.0 / math.pi) * (x + 0.044715 * (x * x * x))))


def _ada_kernel(c_ref, w_ref, b_ref, o_ref):
    c = c_ref[...]
    s = c * jax.nn.sigmoid(c)
    o_ref[...] = jnp.dot(s, w_ref[...], preferred_element_type=F32) + b_ref[...]


def _ada(c_pad, w, b, tn=512):
    rows, d = c_pad.shape
    n6 = w.shape[1]
    return pl.pallas_call(
        _ada_kernel,
        grid=(n6 // tn,),
        in_specs=[
            pl.BlockSpec((rows, d), lambda j: (0, 0)),
            pl.BlockSpec((d, tn), lambda j: (0, j)),
            pl.BlockSpec((1, tn), lambda j: (0, j)),
        ],
        out_specs=pl.BlockSpec((rows, tn), lambda j: (0, j)),
        out_shape=jax.ShapeDtypeStruct((rows, n6), F32),
        compiler_params=_cparams(("arbitrary",)),
        name="ada",
    )(c_pad, w, b)


def _inproj_kernel(x_ref, mod_ref, w_ref, o_ref, u_ref):
    @pl.when(pl.program_id(1) == 0)
    def _():
        m = mod_ref[0]
        u = _ln(x_ref[...]) * (1.0 + m[1:2]) + m[0:1]
        u_ref[...] = u.astype(BF16)

    o_ref[...] = jnp.dot(u_ref[...], w_ref[...], preferred_element_type=F32)


def _inproj(x, mod, w, seq, tm, tn):
    n, d = x.shape
    nout = w.shape[1]
    return pl.pallas_call(
        _inproj_kernel,
        grid=(n // tm, nout // tn),
        in_specs=[
            pl.BlockSpec((tm, d), lambda i, j: (i, 0)),
            pl.BlockSpec((1, 6, d), lambda i, j: (i * tm // seq, 0, 0)),
            pl.BlockSpec((d, tn), lambda i, j: (0, j)),
        ],
        out_specs=pl.BlockSpec((tm, tn), lambda i, j: (i, j)),
        out_shape=jax.ShapeDtypeStruct((n, nout), F32),
        scratch_shapes=[pltpu.VMEM((tm, d), BF16)],
        compiler_params=_cparams(("arbitrary", "arbitrary")),
        name="inproj",
    )(x, mod, w)


def _lru_kernel(xl_ref, gl_ref, cw_ref, cb_ref, wa_ref, ba_ref, wx_ref, bx_ref, lam_ref,
                o_ref, xe_ref, a_ref, b_ref, h_ref, *, nblk):
    tc, c = xl_ref.shape

    @pl.when(pl.program_id(1) == 0)
    def _():
        xe_ref[0:SUBLANES, :] = jnp.zeros((SUBLANES, c), F32)
        h_ref[...] = jnp.zeros_like(h_ref)

    x = xl_ref[...]
    xe_ref[SUBLANES:SUBLANES + tc, :] = x
    cw = cw_ref[...]
    xc = (x * cw[3:4]
          + xe_ref[pl.ds(SUBLANES - 1, tc), :] * cw[2:3]
          + xe_ref[pl.ds(SUBLANES - 2, tc), :] * cw[1:2]
          + xe_ref[pl.ds(SUBLANES - 3, tc), :] * cw[0:1]
          + cb_ref[...])
    xe_ref[0:SUBLANES, :] = x[tc - SUBLANES:tc, :]

    lam = lam_ref[...]
    z = -lam
    softplus = jnp.maximum(z, 0.0) + jnp.log1p(jnp.exp(-jnp.abs(z)))
    for nb in range(nblk):
        sl = slice(nb * LANES, (nb + 1) * LANES)
        xb = xc[:, sl]
        xb16 = xb.astype(BF16)
        r = jax.nn.sigmoid(jnp.dot(xb16, wa_ref[nb], preferred_element_type=F32) + ba_ref[:, sl])
        i = jax.nn.sigmoid(jnp.dot(xb16, wx_ref[nb], preferred_element_type=F32) + bx_ref[:, sl])
        log_a = (-LRU_C) * r * softplus[:, sl]
        a = jnp.exp(log_a)
        one_m_a2 = -jnp.tanh(log_a) * (a * a + 1.0)
        a_ref[:, sl] = a
        b_ref[:, sl] = jnp.sqrt(one_m_a2) * (i * xb)

    row = lax.broadcasted_iota(I32, (SUBLANES, c), 0)

    def body(g, h):
        r0 = pl.multiple_of(g * SUBLANES, SUBLANES)
        a = a_ref[pl.ds(r0, SUBLANES), :]
        b = b_ref[pl.ds(r0, SUBLANES), :]
        for d in (1, 2, 4):
            keep = row >= d
            a_s = pltpu.roll(a, d, 0)
            b_s = pltpu.roll(b, d, 0)
            b = jnp.where(keep, a * b_s + b, b)
            a = jnp.where(keep, a * a_s, a)
        hh = a * h + b
        o_ref[pl.ds(r0, SUBLANES), :] = (hh * _gelu(gl_ref[pl.ds(r0, SUBLANES), :])).astype(o_ref.dtype)
        return hh[SUBLANES - 1:SUBLANES, :]

    h_ref[0:1, :] = lax.fori_loop(0, tc // SUBLANES, body, h_ref[0:1, :])


def _lru(proj, cw, cb, wa, ba, wx, bx, lam, batch, seq, tc):
    n = proj.shape[0]
    c = cw.shape[1]
    nblk = wa.shape[0]
    nt = seq // tc
    full = lambda shape: pl.BlockSpec(shape, lambda b, t: (0,) * len(shape))
    return pl.pallas_call(
        functools.partial(_lru_kernel, nblk=nblk),
        grid=(batch, nt),
        in_specs=[
            pl.BlockSpec((tc, c), lambda b, t: (b * nt + t, 0)),
            pl.BlockSpec((tc, c), lambda b, t: (b * nt + t, 1)),
            full((CONV_W, c)), full((1, c)),
            full((nblk, LANES, LANES)), full((1, c)),
            full((nblk, LANES, LANES)), full((1, c)),
            full((1, c)),
        ],
        out_specs=pl.BlockSpec((tc, c), lambda b, t: (b * nt + t, 0)),
        out_shape=jax.ShapeDtypeStruct((n, c), BF16),
        scratch_shapes=[
            pltpu.VMEM((tc + SUBLANES, c), F32),
            pltpu.VMEM((tc, c), F32),
            pltpu.VMEM((tc, c), F32),
            pltpu.VMEM((SUBLANES, c), F32),
        ],
        compiler_params=_cparams(("arbitrary", "arbitrary")),
        name="lru",
    )(proj, proj, cw, cb, wa, ba, wx, bx, lam)


def _rope_tables(ang, lane_in_head, half):
    cos = jnp.cos(ang)
    sin = jnp.sin(ang)
    c = jnp.where(lane_in_head < 2 * half, cos, 1.0)
    s1 = jnp.where(lane_in_head < half, -sin, 0.0)
    s2 = jnp.where((lane_in_head >= half) & (lane_in_head < 2 * half), sin, 0.0)
    return c, s1, s2


def _rope(x, tabs, half):
    c, s1, s2 = tabs
    return x * c + pltpu.roll(x, LANES - half, 1) * s1 + pltpu.roll(x, half, 1) * s2


def _dsaprep_kernel(q_ref, k_ref, v_ref, qi_ref, kw_ref, pos_ref, g_ref, fa_ref, fi_ref,
                    qo_ref, ko_ref, vo_ref, qio_ref, kdo_ref, wo_ref, *, nh, npair, wscale):
    tp = q_ref.shape[0]
    pos = pos_ref[...].astype(F32)
    lane = lax.broadcasted_iota(I32, (tp, LANES), 1)
    tab_a = _rope_tables(pos * fa_ref[...], lane, ROT_ATT // 2)
    tab_i = _rope_tables(pos * fi_ref[...], lane % IDX_DIM, ROT_IDX // 2)
    for h in range(nh):
        sl = slice(h * LANES, (h + 1) * LANES)
        qo_ref[:, sl] = _rope(q_ref[:, sl], tab_a, ROT_ATT // 2).astype(BF16)
        ko_ref[:, sl] = _rope(k_ref[:, sl], tab_a, ROT_ATT // 2).astype(BF16)
    vo_ref[...] = v_ref[...].astype(BF16)
    first = lane < IDX_DIM
    for p in range(npair):
        r = _rope(qi_ref[:, p * LANES:(p + 1) * LANES], tab_i, ROT_IDX // 2)
        qio_ref[:, (2 * p) * LANES:(2 * p + 1) * LANES] = jnp.where(first, r, 0.0).astype(BF16)
        qio_ref[:, (2 * p + 1) * LANES:(2 * p + 2) * LANES] = jnp.where(first, 0.0, r).astype(BF16)
    kw = kw_ref[...]
    mu = jnp.sum(jnp.where(first, kw, 0.0), axis=1, keepdims=True) * (1.0 / IDX_DIM)
    kc = jnp.where(first, kw - mu, 0.0)
    var = jnp.sum(kc * kc, axis=1, keepdims=True) * (1.0 / IDX_DIM)
    kn = kc * lax.rsqrt(var + LN_EPS) * g_ref[...]
    kr = _rope(kn, tab_i, ROT_IDX // 2)
    kr = jnp.where(first, kr, 0.0)
    kdo_ref[...] = (kr + pltpu.roll(kr, IDX_DIM, 1)).astype(BF16)
    wo_ref[...] = kw * wscale


def _dsaprep(proj, pos, g_pad, fa, fi, d_lru, d_att, ni, tp):
    n = proj.shape[0]
    nh = d_att // HEAD_DIM
    npair = ni * IDX_DIM // LANES
    assert d_lru % d_att == 0 or d_att % d_lru == 0
    q0 = 2 * d_lru // d_att
    kw_blk = (2 * d_lru + 3 * d_att + ni * IDX_DIM) // LANES
    assert (ni * IDX_DIM) % d_att == 0 or ni * IDX_DIM == d_att
    wide = lambda j: pl.BlockSpec((tp, d_att), lambda i: (i, j))
    row1 = lambda: pl.BlockSpec((1, LANES), lambda i: (0, 0))
    out_w = lambda: pl.BlockSpec((tp, d_att), lambda i: (i, 0))
    out_n = lambda: pl.BlockSpec((tp, LANES), lambda i: (i, 0))
    wscale = float(ni) ** -0.5 * float(IDX_DIM) ** -0.5
    return pl.pallas_call(
        functools.partial(_dsaprep_kernel, nh=nh, npair=npair, wscale=wscale),
        grid=(n // tp,),
        in_specs=[
            wide(q0), wide(q0 + 1), wide(q0 + 2),
            pl.BlockSpec((tp, ni * IDX_DIM), lambda i: (i, (2 * d_lru + 3 * d_att) // (ni * IDX_DIM))),
            pl.BlockSpec((tp, LANES), lambda i: (i, kw_blk)),
            pl.BlockSpec((tp, 1), lambda i: (i, 0)),
            row1(), row1(), row1(),
        ],
        out_specs=[out_w(), out_w(), out_w(),
                   pl.BlockSpec((tp, ni * LANES), lambda i: (i, 0)), out_n(), out_n()],
        out_shape=[
            jax.ShapeDtypeStruct((n, d_att), BF16),
            jax.ShapeDtypeStruct((n, d_att), BF16),
            jax.ShapeDtypeStruct((n, d_att), BF16),
            jax.ShapeDtypeStruct((n, ni * LANES), BF16),
            jax.ShapeDtypeStruct((n, LANES), BF16),
            jax.ShapeDtypeStruct((n, LANES), F32),
        ],
        compiler_params=_cparams(("arbitrary",)),
        name="dsaprep",
    )(proj, proj, proj, proj, proj, pos, g_pad, fa, fi)


def _dsa_kernel(q_ref, qi_ref, w_ref, k_ref, v_ref, kd_ref, o_ref, sc_ref, bias_ref, wb_ref,
                m_ref, l_ref, acc_ref, *, tq, tk, nh, ni, topk, att_scale):
    qb = pl.program_id(1)
    ntile = ((qb + 1) * tq + tk - 1) // tk
    nsub = tk // LANES
    topk_f = float(topk)

    w = w_ref[...]
    for h in range(ni):
        wb_ref[h] = jnp.broadcast_to(w[:, IDX_DIM + h:IDX_DIM + h + 1], (tq, LANES))
    row = lax.broadcasted_iota(I32, (tq, tk), 0)
    lane = lax.broadcasted_iota(I32, (tq, tk), 1)
    bound = (((qb * tq + row) >> CHUNK_SHIFT) + 1) << CHUNK_SHIFT

    n_adm = bound[:, 0:1].astype(F32)

    def score_tile(j, carry):
        mx, mn = carry
        k0 = pl.multiple_of(j * tk, tk)
        kd = kd_ref[pl.ds(k0, tk), :]
        acc = [jnp.zeros((tq, LANES), F32) for _ in range(nsub)]
        for h in range(ni):
            qh = qi_ref[:, h * LANES:(h + 1) * LANES]
            logit = lax.dot_general(qh, kd, _NT, preferred_element_type=F32)
            wh = wb_ref[h]
            for s in range(nsub):
                acc[s] = acc[s] + wh * jnp.maximum(logit[:, s * LANES:(s + 1) * LANES], 0.0)
        score = jnp.concatenate(acc, axis=1)
        adm = (k0 + lane) < bound
        sc_ref[j] = jnp.where(adm, score, -jnp.inf)
        mx = jnp.maximum(mx, jnp.max(jnp.where(adm, score, -jnp.inf), axis=1, keepdims=True))
        mn = jnp.minimum(mn, jnp.min(jnp.where(adm, score, jnp.inf), axis=1, keepdims=True))
        return mx, mn

    mx, mn = lax.fori_loop(0, ntile, score_tile,
                           (jnp.full((tq, 1), -jnp.inf, F32), jnp.full((tq, 1), jnp.inf, F32)))

    def count_ge(mid):
        def body(j, acc):
            ge = jnp.where(sc_ref[j] >= mid, 1.0, 0.0)
            for s in range(nsub):
                acc = acc + ge[:, s * LANES:(s + 1) * LANES]
            return acc
        acc = lax.fori_loop(0, ntile, body, jnp.zeros((tq, LANES), F32))
        return jnp.sum(acc, axis=1, keepdims=True)

    small = n_adm <= topk_f

    def pending(resolved):
        return jnp.max(jnp.where(resolved, 0.0, 1.0))

    def bisect_cond(c):
        return (c[0] < MAX_BISECT) & (c[4] > 0.0)

    def bisect(c):
        it, lo, hi, cnt_lo, _ = c
        mid = 0.5 * lo + 0.5 * hi
        cnt = count_ge(mid)
        ge = cnt >= topk_f
        adjacent = (mid <= lo) | (mid >= hi)
        lo = jnp.where(ge, mid, lo)
        cnt_lo = jnp.where(ge, cnt, cnt_lo)
        hi = jnp.where(ge, hi, mid)
        return it + 1, lo, hi, cnt_lo, pending((cnt_lo == topk_f) | small | adjacent)

    _, thr, _, _, _ = lax.while_loop(
        bisect_cond, bisect, (jnp.int32(0), mn, mx, n_adm, pending((n_adm == topk_f) | small)))
    thr = jnp.where(small, -jnp.inf, thr)

    def bias_tile(j, carry):
        k0 = pl.multiple_of(j * tk, tk)
        sel = (sc_ref[j] >= thr) & ((k0 + lane) < bound)
        bias_ref[j] = jnp.where(sel, 0.0, NEG_BIG)
        return carry

    lax.fori_loop(0, ntile, bias_tile, 0)

    for h in range(nh):
        m_ref[h] = jnp.full((tq, LANES), NEG_BIG, F32)
        l_ref[h] = jnp.zeros((tq, LANES), F32)
        acc_ref[h] = jnp.zeros((tq, HEAD_DIM), F32)

    def att_tile(j, carry):
        k0 = pl.multiple_of(j * tk, tk)
        bias = bias_ref[j]
        heads_sl = [slice(h * HEAD_DIM, (h + 1) * HEAD_DIM) for h in range(nh)]
        qk = [lax.dot_general(q_ref[:, sl], k_ref[pl.ds(k0, tk), sl], _NT, preferred_element_type=F32)
              for sl in heads_sl]
        for h, sl in enumerate(heads_sl):
            vt = v_ref[pl.ds(k0, tk), sl]
            s = qk[h] * att_scale + bias
            m_old = m_ref[h]
            m_new = jnp.maximum(m_old, jnp.max(s, axis=1, keepdims=True))
            alpha = jnp.exp(m_old - m_new)
            p = jnp.exp(s - jnp.concatenate([m_new] * nsub, axis=1))
            l_ref[h] = alpha * l_ref[h] + jnp.sum(p, axis=1, keepdims=True)
            acc_ref[h] = alpha * acc_ref[h] + jnp.dot(p.astype(BF16), vt, preferred_element_type=F32)
            m_ref[h] = m_new
        return carry

    lax.fori_loop(0, ntile, att_tile, 0)
    for h in range(nh):
        sl = slice(h * HEAD_DIM, (h + 1) * HEAD_DIM)
        o_ref[:, sl] = (acc_ref[h] / l_ref[h]).astype(o_ref.dtype)


def _dsa(q, k, v, qi, kd, w, batch, seq, tq, tk):
    n, d_att = q.shape
    nh = d_att // HEAD_DIM
    ni = qi.shape[1] // LANES
    topk = min(TOPK_MAX, seq // 4)
    nq = seq // tq
    assert tq % CHUNK == 0 and seq % tq == 0 and seq % tk == 0 and tk % CHUNK == 0 and tk >= topk
    qspec = lambda width: pl.BlockSpec((tq, width), lambda b, i: (b * nq + i, 0))
    kspec = lambda width: pl.BlockSpec((seq, width), lambda b, i: (b, 0), pipeline_mode=pl.Buffered(1))
    return pl.pallas_call(
        functools.partial(_dsa_kernel, tq=tq, tk=tk, nh=nh, ni=ni, topk=topk,
                          att_scale=float(HEAD_DIM) ** -0.5),
        grid=(batch, nq),
        in_specs=[qspec(d_att), qspec(ni * LANES), qspec(LANES),
                  kspec(d_att), kspec(d_att), kspec(LANES)],
        out_specs=qspec(d_att),
        out_shape=jax.ShapeDtypeStruct((n, d_att), BF16),
        scratch_shapes=[
            pltpu.VMEM((seq // tk, tq, tk), F32),
            pltpu.VMEM((seq // tk, tq, tk), F32),
            pltpu.VMEM((ni, tq, LANES), F32),
            pltpu.VMEM((nh, tq, LANES), F32),
            pltpu.VMEM((nh, tq, LANES), F32),
            pltpu.VMEM((nh, tq, HEAD_DIM), F32),
        ],
        compiler_params=_cparams(("arbitrary", "arbitrary")),
        name="dsa",
    )(q, qi, w, k, v, kd)


def _outproj_kernel(yl_ref, ya_ref, x_ref, mod_ref, wl_ref, wa_ref, g_ref, b_ref,
                    x1_ref, u2_ref, *, alpha):
    m = mod_ref[0]
    mix = (jnp.dot(yl_ref[...], wl_ref[...], preferred_element_type=F32)
           + jnp.dot(ya_ref[...], wa_ref[...], preferred_element_type=F32))
    x1 = _ln(alpha * x_ref[...] + (1.0 + m[2:3]) * mix) * g_ref[...] + b_ref[...]
    x1_ref[...] = x1
    u2_ref[...] = pltpu.bitcast((_ln(x1) * (1.0 + m[4:5]) + m[3:4]).astype(BF16), I32)


def _outproj(yl, ya, x, mod, wl, wa, g, b, seq, alpha, tm):
    n, d = x.shape
    row = lambda width: pl.BlockSpec((tm, width), lambda i: (i, 0))
    full = lambda shape: pl.BlockSpec(shape, lambda i: (0,) * len(shape))
    return pl.pallas_call(
        functools.partial(_outproj_kernel, alpha=alpha),
        grid=(n // tm,),
        in_specs=[row(yl.shape[1]), row(ya.shape[1]), row(d),
                  pl.BlockSpec((1, 6, d), lambda i: (i * tm // seq, 0, 0)),
                  full(wl.shape), full(wa.shape), full((1, d)), full((1, d))],
        out_specs=[row(d), pl.BlockSpec((tm // 2, d), lambda i: (i, 0))],
        out_shape=[jax.ShapeDtypeStruct((n, d), F32), jax.ShapeDtypeStruct((n // 2, d), I32)],
        compiler_params=_cparams(("arbitrary",)),
        name="outproj",
    )(yl, ya, x, mod, wl, wa, g, b)


def _topk_desc(x, k):
    out = []
    for _ in range(k):
        m = jnp.max(x, axis=0, keepdims=True)
        out.append(m)
        x = jnp.where(x == m, -jnp.inf, x)
    return out


def _route_kernel(u_ref, wpq_ref, sk_ref, s1_ref, e1_ref, s2_ref, e2_ref, tau_ref,
                  v1_ref, v2_ref, cand_ref, *, heads):
    tm = 2 * u_ref.shape[0]
    nlt = tm // LANES
    qp = jnp.dot(pltpu.bitcast(u_ref[...], BF16), wpq_ref[...], preferred_element_type=F32).astype(BF16)
    cand_ref[...] = jnp.full(cand_ref.shape, -jnp.inf, F32)
    for h in range(heads):
        st = []
        for p in range(2):
            col = (2 * h + p) * LANES
            st.append(lax.dot_general(sk_ref[h, p], qp[:, col:col + LANES], _NT,
                                      preferred_element_type=F32))
        s1, s2 = st
        for k, m in enumerate(_topk_desc(s1, PEER_TOPK)):
            v1_ref[k:k + 1, :] = m
        for k, m in enumerate(_topk_desc(s2, PEER_TOPK)):
            v2_ref[k:k + 1, :] = m
        off = 0
        for i in range(PEER_TOPK):
            nj = PEER_TOPK // (i + 1)
            cand_ref[off:off + nj, :] = v1_ref[i:i + 1, :] + v2_ref[0:nj, :]
            off += nj
        m1 = v1_ref[0:1, :]
        m2 = v2_ref[0:1, :]
        top = _topk_desc(cand_ref[...], PEER_TOPK)
        tau = top[-1]
        z = top[0] * 0.0
        for c in top:
            z = z + jnp.exp(c - (m1 + m2))
        e1 = jnp.exp(s1 - m1) / z
        e2 = jnp.exp(s2 - m2)
        tau8 = jnp.broadcast_to(tau, (SUBLANES, tm))
        for lt in range(nlt):
            ls = slice(lt * LANES, (lt + 1) * LANES)
            s1_ref[h, lt] = s1[:, ls]
            e1_ref[h, lt] = e1[:, ls]
            s2_ref[h, lt] = s2[:, ls]
            e2_ref[h, lt] = e2[:, ls]
            tau_ref[h, lt] = tau8[:, ls]


def _route(u2, wpq, sk, tm):
    n, d = 2 * u2.shape[0], u2.shape[1]
    heads, _, nkeys, kd = sk.shape
    assert nkeys == LANES and kd == LANES
    nt = n // LANES
    nlt = tm // LANES
    ncand = sum(PEER_TOPK // (i + 1) for i in range(PEER_TOPK))
    ncand_pad = -(-ncand // SUBLANES) * SUBLANES
    tile = lambda: pl.BlockSpec((heads, nlt, nkeys, LANES), lambda i: (0, i, 0, 0))
    shp = jax.ShapeDtypeStruct((heads, nt, nkeys, LANES), F32)
    return pl.pallas_call(
        functools.partial(_route_kernel, heads=heads),
        grid=(n // tm,),
        in_specs=[
            pl.BlockSpec((tm // 2, d), lambda i: (i, 0)),
            pl.BlockSpec(wpq.shape, lambda i: (0, 0)),
            pl.BlockSpec(sk.shape, lambda i: (0, 0, 0, 0)),
        ],
        out_specs=[tile(), tile(), tile(), tile(),
                   pl.BlockSpec((heads, nlt, SUBLANES, LANES), lambda i: (0, i, 0, 0))],
        out_shape=[shp, shp, shp, shp,
                   jax.ShapeDtypeStruct((heads, nt, SUBLANES, LANES), F32)],
        scratch_shapes=[
            pltpu.VMEM((PEER_TOPK, tm), F32),
            pltpu.VMEM((PEER_TOPK, tm), F32),
            pltpu.VMEM((ncand_pad, tm), F32),
        ],
        compiler_params=_cparams(("arbitrary",)),
        name="route",
    )(u2, wpq, sk)


def _peer_kernel(u_ref, ut_ref, vt_ref, s1_ref, e1_ref, s2_ref, e2_ref, tau_ref,
                 o_ref, act_ref, hk_ref, *, heads, na):
    tm = 2 * u_ref.shape[0]
    nlt = tm // LANES

    @pl.when(pl.program_id(1) == 0)
    def _():
        o_ref[...] = jnp.zeros_like(o_ref)

    act_ref[...] = lax.dot_general(ut_ref[...], pltpu.bitcast(u_ref[...], BF16), _NT,
                                   preferred_element_type=F32)

    def gate_rows(a, carry):
        r0 = pl.multiple_of(a * LANES, LANES)
        for lt in range(nlt):
            ls = slice(lt * LANES, (lt + 1) * LANES)
            g = jnp.zeros((LANES, LANES), F32)
            for h in range(heads):
                s1 = s1_ref[h, lt, pl.ds(a, 1), :]
                e1 = e1_ref[h, lt, pl.ds(a, 1), :]
                tau = tau_ref[h, lt, 0:1, :]
                g = g + jnp.where(s1 + s2_ref[h, lt] >= tau, e1 * e2_ref[h, lt], 0.0)
            hk_ref[pl.ds(r0, LANES), ls] = (g * _gelu(act_ref[pl.ds(r0, LANES), ls])).astype(BF16)
        return carry

    lax.fori_loop(0, na, gate_rows, 0)
    o_ref[...] += lax.dot_general(hk_ref[...], vt_ref[...], _TN, preferred_element_type=F32)


def _peer(u2, ut, vt, s1, e1, s2, e2, tau, tm, te):
    n, d = 2 * u2.shape[0], u2.shape[1]
    ne = vt.shape[0]
    heads = s1.shape[0]
    nlt = tm // LANES
    na = te // LANES
    assert na == SUBLANES
    tile = lambda rows: pl.BlockSpec((heads, nlt, rows, LANES), lambda i, j: (0, i, 0, 0))
    arow = lambda: pl.BlockSpec((heads, nlt, na, LANES), lambda i, j: (0, i, j, 0))
    return pl.pallas_call(
        functools.partial(_peer_kernel, heads=heads, na=na),
        grid=(n // tm, ne // te),
        in_specs=[
            pl.BlockSpec((tm // 2, d), lambda i, j: (i, 0)),
            pl.BlockSpec((te, d), lambda i, j: (j, 0)),
            pl.BlockSpec((te, d), lambda i, j: (j, 0)),
            arow(), arow(), tile(LANES), tile(LANES), tile(SUBLANES),
        ],
        out_specs=pl.BlockSpec((tm, d), lambda i, j: (i, 0)),
        out_shape=jax.ShapeDtypeStruct((n, d), F32),
        scratch_shapes=[pltpu.VMEM((te, tm), F32), pltpu.VMEM((te, tm), BF16)],
        compiler_params=_cparams(("arbitrary", "arbitrary")),
        name="peer",
    )(u2, ut, vt, s1, e1, s2, e2, tau)


def _final_kernel(x1_ref, y_ref, mod_ref, g_ref, b_ref, o_ref, *, alpha):
    m = mod_ref[0]
    o_ref[...] = _ln(alpha * x1_ref[...] + (1.0 + m[5:6]) * y_ref[...]) * g_ref[...] + b_ref[...]


def _final(x1, y, mod, g, b, seq, alpha, tm):
    n, d = x1.shape
    row = lambda: pl.BlockSpec((tm, d), lambda i: (i, 0))
    full = lambda: pl.BlockSpec((1, d), lambda i: (0, 0))
    return pl.pallas_call(
        functools.partial(_final_kernel, alpha=alpha),
        grid=(n // tm,),
        in_specs=[row(), row(), pl.BlockSpec((1, 6, d), lambda i: (i * tm // seq, 0, 0)),
                  full(), full()],
        out_specs=row(),
        out_shape=jax.ShapeDtypeStruct((n, d), F32),
        compiler_params=_cparams(("arbitrary",)),
        name="final",
    )(x1, y, mod, g, b)


def _tile(total, want):
    t = min(total, want)
    while total % t:
        t //= 2
    return t


def _freq_row(rot, period):
    half = rot // 2
    inv = ROPE_THETA ** (-jnp.arange(half, dtype=F32) / half)
    lane = np.arange(LANES) % period
    return jnp.where(lane < rot, jnp.tile(inv, LANES // half), 0.0)[None, :]


def kernel(x, c, positions, w_ada, b_ada, w_in, conv_w, conv_b, w_rg_a, b_rg_a, w_rg_x, b_rg_x,
           lru_lambda, k_idx_g, w_out, ln1_g, ln1_b, w_pq, peer_sub_keys, peer_u, peer_v,
           ln2_g, ln2_b):
    batch, seq, d = x.shape
    depth = w_ada.shape[0]
    n = batch * seq
    d_lru = conv_w.shape[-1]
    d_att = w_out.shape[1] - d_lru
    d_in = w_in.shape[-1]
    ni = (d_in - 2 * d_lru - 3 * d_att - IDX_DIM) // (IDX_DIM + 1)
    assert 2 * d_lru + 3 * d_att + ni * IDX_DIM + IDX_DIM + ni == d_in
    alpha = (2.0 * depth) ** 0.25

    nout = -(-d_in // (2 * LANES)) * (2 * LANES)
    tn_in = _tile(nout, 1280) if nout % 1280 == 0 else 2 * LANES
    pos = positions.reshape(n, 1).astype(I32)
    fa = _freq_row(ROT_ATT, HEAD_DIM)
    fi = _freq_row(ROT_IDX, IDX_DIM)
    c_pad = jnp.pad(c, ((0, SUBLANES - batch), (0, 0)))

    xf = x.reshape(n, d)
    for l in range(depth):
        mod = _ada(c_pad, w_ada[l], b_ada[l][None, :])[:batch].reshape(batch, 6, d)
        w_in_p = jnp.pad(w_in[l], ((0, 0), (0, nout - d_in))).astype(BF16)
        proj = _inproj(xf, mod, w_in_p, seq, _tile(seq, 1024), tn_in)
        y_lru = _lru(proj, conv_w[l], conv_b[l][None, :], w_rg_a[l].astype(BF16), b_rg_a[l][None, :],
                     w_rg_x[l].astype(BF16), b_rg_x[l][None, :], lru_lambda[l][None, :],
                     batch, seq, _tile(seq, 256))
        g_pad = jnp.pad(k_idx_g[l], (0, LANES - IDX_DIM))[None, :]
        q, k, v, qi, kd, w = _dsaprep(proj, pos, g_pad, fa, fi, d_lru, d_att, ni, _tile(seq, 512))
        y_att = _dsa(q, k, v, qi, kd, w, batch, seq, _tile(seq, 2 * Q_BLOCK), _tile(seq, 512))
        wo = w_out[l].astype(BF16)
        x1, u2 = _outproj(y_lru, y_att, xf, mod, wo[:d_lru], wo[d_lru:], ln1_g[l][None, :],
                          ln1_b[l][None, :], seq, alpha, _tile(seq, 256))
        s1, e1, s2, e2, tau = _route(u2, w_pq[l].astype(BF16), peer_sub_keys[l].astype(BF16),
                                     _tile(seq, 256))
        y = _peer(u2, peer_u[l].astype(BF16), peer_v[l].astype(BF16), s1, e1, s2, e2, tau,
                  _tile(seq, 512), SUBLANES * LANES)
        xf = _final(x1, y, mod, ln2_g[l][None, :], ln2_b[l][None, :], seq, alpha, _tile(seq, 512))
    return xf.reshape(batch, seq, d)
```

```python
import functools
import math

import jax
import jax.numpy as jnp
import numpy as np
from jax import lax
from jax.experimental import pallas as pl
from jax.experimental.pallas import tpu as pltpu

F32 = jnp.float32
BF16 = jnp.bfloat16
I32 = jnp.int32

LANES = 128
SUBLANES = 8
VMEM_LIMIT = 56 * 1024 * 1024

LN_EPS = 1e-5
CHUNK = 64
CHUNK_SHIFT = 6
assert 1 << CHUNK_SHIFT == CHUNK
CONV_W = 4
LRU_C = 8.0
HEAD_DIM = 128
IDX_DIM = 64
TOPK_MAX = 256
Q_BLOCK = 128
ROPE_THETA = 500000.0
ROT_ATT = HEAD_DIM // 4
ROT_IDX = IDX_DIM // 4
PEER_TOPK = 16
NEG_BIG = -1e30

MAX_BISECT = 300

_NT = (((1,), (1,)), ((), ()))
_TN = (((0,), (0,)), ((), ()))


def _cparams(sem):
    return pltpu.CompilerParams(dimension_semantics=sem, vmem_limit_bytes=VMEM_LIMIT)


def _ln(x):
    mu = jnp.mean(x, axis=-1, keepdims=True)
    xc = x - mu
    var = jnp.mean(xc * xc, axis=-1, keepdims=True)
    return xc * lax.rsqrt(var + LN_EPS)


def _gelu(x):
    return 0.5 * x * (1.0 + jnp.tanh(math.sqrt(2.0 / math.pi) * (x + 0.044715 * (x * x * x))))


def _ada_kernel(c_ref, w_ref, b_ref, o_ref):
    c = c_ref[...]
    s = c * jax.nn.sigmoid(c)
    o_ref[...] = jnp.dot(s, w_ref[...], preferred_element_type=F32) + b_ref[...]


def _ada(c_pad, w, b, tn=512):
    rows, d = c_pad.shape
    n6 = w.shape[1]
    return pl.pallas_call(
        _ada_kernel,
        grid=(n6 // tn,),
        in_specs=[
            pl.BlockSpec((rows, d), lambda j: (0, 0)),
            pl.BlockSpec((d, tn), lambda j: (0, j)),
            pl.BlockSpec((1, tn), lambda j: (0, j)),
        ],
        out_specs=pl.BlockSpec((rows, tn), lambda j: (0, j)),
        out_shape=jax.ShapeDtypeStruct((rows, n6), F32),
        compiler_params=_cparams(("arbitrary",)),
        name="ada",
    )(c_pad, w, b)


def _inproj_kernel(x_ref, mod_ref, w_ref, o_ref, u_ref):
    @pl.when(pl.program_id(1) == 0)
    def _():
        m = mod_ref[0]
        u = _ln(x_ref[...]) * (1.0 + m[1:2]) + m[0:1]
        u_ref[...] = u.astype(BF16)

    o_ref[...] = jnp.dot(u_ref[...], w_ref[...], preferred_element_type=F32)


def _inproj(x, mod, w, seq, tm, tn):
    n, d = x.shape
    nout = w.shape[1]
    return pl.pallas_call(
        _inproj_kernel,
        grid=(n // tm, nout // tn),
        in_specs=[
            pl.BlockSpec((tm, d), lambda i, j: (i, 0)),
            pl.BlockSpec((1, 6, d), lambda i, j: (i * tm // seq, 0, 0)),
            pl.BlockSpec((d, tn), lambda i, j: (0, j)),
        ],
        out_specs=pl.BlockSpec((tm, tn), lambda i, j: (i, j)),
        out_shape=jax.ShapeDtypeStruct((n, nout), F32),
        scratch_shapes=[pltpu.VMEM((tm, d), BF16)],
        compiler_params=_cparams(("arbitrary", "arbitrary")),
        name="inproj",
    )(x, mod, w)


def _lru_kernel(xl_ref, gl_ref, cw_ref, cb_ref, wa_ref, ba_ref, wx_ref, bx_ref, lam_ref,
                o_ref, xe_ref, a_ref, b_ref, h_ref, *, nblk):
    tc, c = xl_ref.shape

    @pl.when(pl.program_id(1) == 0)
    def _():
        xe_ref[0:SUBLANES, :] = jnp.zeros((SUBLANES, c), F32)
        h_ref[...] = jnp.zeros_like(h_ref)

    x = xl_ref[...]
    xe_ref[SUBLANES:SUBLANES + tc, :] = x
    cw = cw_ref[...]
    xc = (x * cw[3:4]
          + xe_ref[pl.ds(SUBLANES - 1, tc), :] * cw[2:3]
          + xe_ref[pl.ds(SUBLANES - 2, tc), :] * cw[1:2]
          + xe_ref[pl.ds(SUBLANES - 3, tc), :] * cw[0:1]
          + cb_ref[...])
    xe_ref[0:SUBLANES, :] = x[tc - SUBLANES:tc, :]

    lam = lam_ref[...]
    z = -lam
    softplus = jnp.maximum(z, 0.0) + jnp.log1p(jnp.exp(-jnp.abs(z)))
    for nb in range(nblk):
        sl = slice(nb * LANES, (nb + 1) * LANES)
        xb = xc[:, sl]
        xb16 = xb.astype(BF16)
        r = jax.nn.sigmoid(jnp.dot(xb16, wa_ref[nb], preferred_element_type=F32) + ba_ref[:, sl])
        i = jax.nn.sigmoid(jnp.dot(xb16, wx_ref[nb], preferred_element_type=F32) + bx_ref[:, sl])
        log_a = (-LRU_C) * r * softplus[:, sl]
        a = jnp.exp(log_a)
        one_m_a2 = -jnp.tanh(log_a) * (a * a + 1.0)
        a_ref[:, sl] = a
        b_ref[:, sl] = jnp.sqrt(one_m_a2) * (i * xb)

    row = lax.broadcasted_iota(I32, (SUBLANES, c), 0)

    def body(g, h):
        r0 = pl.multiple_of(g * SUBLANES, SUBLANES)
        a = a_ref[pl.ds(r0, SUBLANES), :]
        b = b_ref[pl.ds(r0, SUBLANES), :]
        for d in (1, 2, 4):
            keep = row >= d
            a_s = pltpu.roll(a, d, 0)
            b_s = pltpu.roll(b, d, 0)
            b = jnp.where(keep, a * b_s + b, b)
            a = jnp.where(keep, a * a_s, a)
        hh = a * h + b
        o_ref[pl.ds(r0, SUBLANES), :] = (hh * _gelu(gl_ref[pl.ds(r0, SUBLANES), :])).astype(o_ref.dtype)
        return hh[SUBLANES - 1:SUBLANES, :]

    h_ref[0:1, :] = lax.fori_loop(0, tc // SUBLANES, body, h_ref[0:1, :])


def _lru(proj, cw, cb, wa, ba, wx, bx, lam, batch, seq, tc):
    n = proj.shape[0]
    c = cw.shape[1]
    nblk = wa.shape[0]
    nt = seq // tc
    full = lambda shape: pl.BlockSpec(shape, lambda b, t: (0,) * len(shape))
    return pl.pallas_call(
        functools.partial(_lru_kernel, nblk=nblk),
        grid=(batch, nt),
        in_specs=[
            pl.BlockSpec((tc, c), lambda b, t: (b * nt + t, 0)),
            pl.BlockSpec((tc, c), lambda b, t: (b * nt + t, 1)),
            full((CONV_W, c)), full((1, c)),
            full((nblk, LANES, LANES)), full((1, c)),
            full((nblk, LANES, LANES)), full((1, c)),
            full((1, c)),
        ],
        out_specs=pl.BlockSpec((tc, c), lambda b, t: (b * nt + t, 0)),
        out_shape=jax.ShapeDtypeStruct((n, c), BF16),
        scratch_shapes=[
            pltpu.VMEM((tc + SUBLANES, c), F32),
            pltpu.VMEM((tc, c), F32),
            pltpu.VMEM((tc, c), F32),
            pltpu.VMEM((SUBLANES, c), F32),
        ],
        compiler_params=_cparams(("arbitrary", "arbitrary")),
        name="lru",
    )(proj, proj, cw, cb, wa, ba, wx, bx, lam)


def _rope_tables(ang, lane_in_head, half):
    cos = jnp.cos(ang)
    sin = jnp.sin(ang)
    c = jnp.where(lane_in_head < 2 * half, cos, 1.0)
    s1 = jnp.where(lane_in_head < half, -sin, 0.0)
    s2 = jnp.where((lane_in_head >= half) & (lane_in_head < 2 * half), sin, 0.0)
    return c, s1, s2


def _rope(x, tabs, half):
    c, s1, s2 = tabs
    return x * c + pltpu.roll(x, LANES - half, 1) * s1 + pltpu.roll(x, half, 1) * s2


def _dsaprep_kernel(q_ref, k_ref, v_ref, qi_ref, kw_ref, pos_ref, g_ref, fa_ref, fi_ref,
                    qo_ref, ko_ref, vo_ref, qio_ref, kdo_ref, wo_ref, *, nh, npair, wscale):
    tp = q_ref.shape[0]
    pos = pos_ref[...].astype(F32)
    lane = lax.broadcasted_iota(I32, (tp, LANES), 1)
    tab_a = _rope_tables(pos * fa_ref[...], lane, ROT_ATT // 2)
    tab_i = _rope_tables(pos * fi_ref[...], lane % IDX_DIM, ROT_IDX // 2)
    for h in range(nh):
        sl = slice(h * LANES, (h + 1) * LANES)
        qo_ref[:, sl] = _rope(q_ref[:, sl], tab_a, ROT_ATT // 2).astype(BF16)
        ko_ref[:, sl] = _rope(k_ref[:, sl], tab_a, ROT_ATT // 2).astype(BF16)
    vo_ref[...] = v_ref[...].astype(BF16)
    first = lane < IDX_DIM
    for p in range(npair):
        r = _rope(qi_ref[:, p * LANES:(p + 1) * LANES], tab_i, ROT_IDX // 2)
        qio_ref[:, (2 * p) * LANES:(2 * p + 1) * LANES] = jnp.where(first, r, 0.0).astype(BF16)
        qio_ref[:, (2 * p + 1) * LANES:(2 * p + 2) * LANES] = jnp.where(first, 0.0, r).astype(BF16)
    kw = kw_ref[...]
    mu = jnp.sum(jnp.where(first, kw, 0.0), axis=1, keepdims=True) * (1.0 / IDX_DIM)
    kc = jnp.where(first, kw - mu, 0.0)
    var = jnp.sum(kc * kc, axis=1, keepdims=True) * (1.0 / IDX_DIM)
    kn = kc * lax.rsqrt(var + LN_EPS) * g_ref[...]
    kr = _rope(kn, tab_i, ROT_IDX // 2)
    kr = jnp.where(first, kr, 0.0)
    kdo_ref[...] = (kr + pltpu.roll(kr, IDX_DIM, 1)).astype(BF16)
    wo_ref[...] = kw * wscale


def _dsaprep(proj, pos, g_pad, fa, fi, d_lru, d_att, ni, tp):
    n = proj.shape[0]
    nh = d_att // HEAD_DIM
    npair = ni * IDX_DIM // LANES
    assert d_lru % d_att == 0 or d_att % d_lru == 0
    q0 = 2 * d_lru // d_att
    kw_blk = (2 * d_lru + 3 * d_att + ni * IDX_DIM) // LANES
    assert (ni * IDX_DIM) % d_att == 0 or ni * IDX_DIM == d_att
    wide = lambda j: pl.BlockSpec((tp, d_att), lambda i: (i, j))
    row1 = lambda: pl.BlockSpec((1, LANES), lambda i: (0, 0))
    out_w = lambda: pl.BlockSpec((tp, d_att), lambda i: (i, 0))
    out_n = lambda: pl.BlockSpec((tp, LANES), lambda i: (i, 0))
    wscale = float(ni) ** -0.5 * float(IDX_DIM) ** -0.5
    return pl.pallas_call(
        functools.partial(_dsaprep_kernel, nh=nh, npair=npair, wscale=wscale),
        grid=(n // tp,),
        in_specs=[
            wide(q0), wide(q0 + 1), wide(q0 + 2),
            pl.BlockSpec((tp, ni * IDX_DIM), lambda i: (i, (2 * d_lru + 3 * d_att) // (ni * IDX_DIM))),
            pl.BlockSpec((tp, LANES), lambda i: (i, kw_blk)),
            pl.BlockSpec((tp, 1), lambda i: (i, 0)),
            row1(), row1(), row1(),
        ],
        out_specs=[out_w(), out_w(), out_w(),
                   pl.BlockSpec((tp, ni * LANES), lambda i: (i, 0)), out_n(), out_n()],
        out_shape=[
            jax.ShapeDtypeStruct((n, d_att), BF16),
            jax.ShapeDtypeStruct((n, d_att), BF16),
            jax.ShapeDtypeStruct((n, d_att), BF16),
            jax.ShapeDtypeStruct((n, ni * LANES), BF16),
            jax.ShapeDtypeStruct((n, LANES), BF16),
            jax.ShapeDtypeStruct((n, LANES), F32),
        ],
        compiler_params=_cparams(("arbitrary",)),
        name="dsaprep",
    )(proj, proj, proj, proj, proj, pos, g_pad, fa, fi)


def _dsa_kernel(q_ref, qi_ref, w_ref, k_ref, v_ref, kd_ref, o_ref, sc_ref, bias_ref, wb_ref,
                m_ref, l_ref, acc_ref, *, tq, tk, nh, ni, topk, att_scale):
    qb = pl.program_id(1)
    ntile = ((qb + 1) * tq + tk - 1) // tk
    nsub = tk // LANES
    topk_f = float(topk)

    w = w_ref[...]
    for h in range(ni):
        wb_ref[h] = jnp.broadcast_to(w[:, IDX_DIM + h:IDX_DIM + h + 1], (tq, LANES))
    row = lax.broadcasted_iota(I32, (tq, tk), 0)
    lane = lax.broadcasted_iota(I32, (tq, tk), 1)
    bound = (((qb * tq + row) >> CHUNK_SHIFT) + 1) << CHUNK_SHIFT

    n_adm = bound[:, 0:1].astype(F32)

    def score_tile(j, carry):
        mx, mn = carry
        k0 = pl.multiple_of(j * tk, tk)
        kd = kd_ref[pl.ds(k0, tk), :]
        acc = [jnp.zeros((tq, LANES), F32) for _ in range(nsub)]
        for h in range(ni):
            qh = qi_ref[:, h * LANES:(h + 1) * LANES]
            logit = lax.dot_general(qh, kd, _NT, preferred_element_type=F32)
            wh = wb_ref[h]
            for s in range(nsub):
                acc[s] = acc[s] + wh * jnp.maximum(logit[:, s * LANES:(s + 1) * LANES], 0.0)
        score = jnp.concatenate(acc, axis=1)
        adm = (k0 + lane) < bound
        sc_ref[j] = jnp.where(adm, score, -jnp.inf)
        mx = jnp.maximum(mx, jnp.max(jnp.where(adm, score, -jnp.inf), axis=1, keepdims=True))
        mn = jnp.minimum(mn, jnp.min(jnp.where(adm, score, jnp.inf), axis=1, keepdims=True))
        return mx, mn

    mx, mn = lax.fori_loop(0, ntile, score_tile,
                           (jnp.full((tq, 1), -jnp.inf, F32), jnp.full((tq, 1), jnp.inf, F32)))

    def count_ge(mid):
        def body(j, acc):
            ge = jnp.where(sc_ref[j] >= mid, 1.0, 0.0)
            for s in range(nsub):
                acc = acc + ge[:, s * LANES:(s + 1) * LANES]
            return acc
        acc = lax.fori_loop(0, ntile, body, jnp.zeros((tq, LANES), F32))
        return jnp.sum(acc, axis=1, keepdims=True)

    small = n_adm <= topk_f

    def pending(resolved):
        return jnp.max(jnp.where(resolved, 0.0, 1.0))

    def bisect_cond(c):
        return (c[0] < MAX_BISECT) & (c[4] > 0.0)

    def bisect(c):
        it, lo, hi, cnt_lo, _ = c
        mid = 0.5 * lo + 0.5 * hi
        cnt = count_ge(mid)
        ge = cnt >= topk_f
        adjacent = (mid <= lo) | (mid >= hi)
        lo = jnp.where(ge, mid, lo)
        cnt_lo = jnp.where(ge, cnt, cnt_lo)
        hi = jnp.where(ge, hi, mid)
        return it + 1, lo, hi, cnt_lo, pending((cnt_lo == topk_f) | small | adjacent)

    _, thr, _, _, _ = lax.while_loop(
        bisect_cond, bisect, (jnp.int32(0), mn, mx, n_adm, pending((n_adm == topk_f) | small)))
    thr = jnp.where(small, -jnp.inf, thr)

    def bias_tile(j, carry):
        k0 = pl.multiple_of(j * tk, tk)
        sel = (sc_ref[j] >= thr) & ((k0 + lane) < bound)
        bias_ref[j] = jnp.where(sel, 0.0, NEG_BIG)
        return carry

    lax.fori_loop(0, ntile, bias_tile, 0)

    for h in range(nh):
        m_ref[h] = jnp.full((tq, LANES), NEG_BIG, F32)
        l_ref[h] = jnp.zeros((tq, LANES), F32)
        acc_ref[h] = jnp.zeros((tq, HEAD_DIM), F32)

    def att_tile(j, carry):
        k0 = pl.multiple_of(j * tk, tk)
        bias = bias_ref[j]
        heads_sl = [slice(h * HEAD_DIM, (h + 1) * HEAD_DIM) for h in range(nh)]
        qk = [lax.dot_general(q_ref[:, sl], k_ref[pl.ds(k0, tk), sl], _NT, preferred_element_type=F32)
              for sl in heads_sl]
        for h, sl in enumerate(heads_sl):
            vt = v_ref[pl.ds(k0, tk), sl]
            s = qk[h] * att_scale + bias
            m_old = m_ref[h]
            m_new = jnp.maximum(m_old, jnp.max(s, axis=1, keepdims=True))
            alpha = jnp.exp(m_old - m_new)
            p = jnp.exp(s - jnp.concatenate([m_new] * nsub, axis=1))
            l_ref[h] = alpha * l_ref[h] + jnp.sum(p, axis=1, keepdims=True)
            acc_ref[h] = alpha * acc_ref[h] + jnp.dot(p.astype(BF16), vt, preferred_element_type=F32)
            m_ref[h] = m_new
        return carry

    lax.fori_loop(0, ntile, att_tile, 0)
    for h in range(nh):
        sl = slice(h * HEAD_DIM, (h + 1) * HEAD_DIM)
        o_ref[:, sl] = (acc_ref[h] / l_ref[h]).astype(o_ref.dtype)


def _dsa(q, k, v, qi, kd, w, batch, seq, tq, tk):
    n, d_att = q.shape
    nh = d_att // HEAD_DIM
    ni = qi.shape[1] // LANES
    topk = min(TOPK_MAX, seq // 4)
    nq = seq // tq
    assert tq % CHUNK == 0 and seq % tq == 0 and seq % tk == 0 and tk % CHUNK == 0 and tk >= topk
    qspec = lambda width: pl.BlockSpec((tq, width), lambda b, i: (b * nq + i, 0))
    kspec = lambda width: pl.BlockSpec((seq, width), lambda b, i: (b, 0), pipeline_mode=pl.Buffered(1))
    return pl.pallas_call(
        functools.partial(_dsa_kernel, tq=tq, tk=tk, nh=nh, ni=ni, topk=topk,
                          att_scale=float(HEAD_DIM) ** -0.5),
        grid=(batch, nq),
        in_specs=[qspec(d_att), qspec(ni * LANES), qspec(LANES),
                  kspec(d_att), kspec(d_att), kspec(LANES)],
        out_specs=qspec(d_att),
        out_shape=jax.ShapeDtypeStruct((n, d_att), BF16),
        scratch_shapes=[
            pltpu.VMEM((seq // tk, tq, tk), F32),
            pltpu.VMEM((seq // tk, tq, tk), F32),
            pltpu.VMEM((ni, tq, LANES), F32),
            pltpu.VMEM((nh, tq, LANES), F32),
            pltpu.VMEM((nh, tq, LANES), F32),
            pltpu.VMEM((nh, tq, HEAD_DIM), F32),
        ],
        compiler_params=_cparams(("arbitrary", "arbitrary")),
        name="dsa",
    )(q, qi, w, k, v, kd)


def _outproj_kernel(yl_ref, ya_ref, x_ref, mod_ref, wl_ref, wa_ref, g_ref, b_ref,
                    x1_ref, u2_ref, *, alpha):
    m = mod_ref[0]
    mix = (jnp.dot(yl_ref[...], wl_ref[...], preferred_element_type=F32)
           + jnp.dot(ya_ref[...], wa_ref[...], preferred_element_type=F32))
    x1 = _ln(alpha * x_ref[...] + (1.0 + m[2:3]) * mix) * g_ref[...] + b_ref[...]
    x1_ref[...] = x1
    u2_ref[...] = pltpu.bitcast((_ln(x1) * (1.0 + m[4:5]) + m[3:4]).astype(BF16), I32)


def _outproj(yl, ya, x, mod, wl, wa, g, b, seq, alpha, tm):
    n, d = x.shape
    row = lambda width: pl.BlockSpec((tm, width), lambda i: (i, 0))
    full = lambda shape: pl.BlockSpec(shape, lambda i: (0,) * len(shape))
    return pl.pallas_call(
        functools.partial(_outproj_kernel, alpha=alpha),
        grid=(n // tm,),
        in_specs=[row(yl.shape[1]), row(ya.shape[1]), row(d),
                  pl.BlockSpec((1, 6, d), lambda i: (i * tm // seq, 0, 0)),
                  full(wl.shape), full(wa.shape), full((1, d)), full((1, d))],
        out_specs=[row(d), pl.BlockSpec((tm // 2, d), lambda i: (i, 0))],
        out_shape=[jax.ShapeDtypeStruct((n, d), F32), jax.ShapeDtypeStruct((n // 2, d), I32)],
        compiler_params=_cparams(("arbitrary",)),
        name="outproj",
    )(yl, ya, x, mod, wl, wa, g, b)


def _topk_desc(x, k):
    out = []
    for _ in range(k):
        m = jnp.max(x, axis=0, keepdims=True)
        out.append(m)
        x = jnp.where(x == m, -jnp.inf, x)
    return out


def _route_kernel(u_ref, wpq_ref, sk_ref, th_ref, e1_ref, s2_ref, e2_ref,
                  v1_ref, v2_ref, cand_ref, *, heads):
    tm = 2 * u_ref.shape[0]
    nlt = tm // LANES
    k1 = PEER_TOPK + 1
    qp = jnp.dot(pltpu.bitcast(u_ref[...], BF16), wpq_ref[...], preferred_element_type=F32).astype(BF16)
    cand_ref[...] = jnp.full(cand_ref.shape, -jnp.inf, F32)
    for h in range(heads):
        st = []
        for p in range(2):
            col = (2 * h + p) * LANES
            st.append(lax.dot_general(sk_ref[h, p], qp[:, col:col + LANES], _NT,
                                      preferred_element_type=F32))
        s1, s2 = st
        for k, m in enumerate(_topk_desc(s1, k1)):
            v1_ref[k:k + 1, :] = m
        for k, m in enumerate(_topk_desc(s2, k1)):
            v2_ref[k:k + 1, :] = m
        off = 0
        for i in range(k1):
            nj = k1 // (i + 1)
            cand_ref[off:off + nj, :] = v1_ref[i:i + 1, :] + v2_ref[0:nj, :]
            off += nj
        m1 = v1_ref[0:1, :]
        m2 = v2_ref[0:1, :]
        top = _topk_desc(cand_ref[...], k1)
        tau = 0.5 * top[PEER_TOPK - 1] + 0.5 * top[PEER_TOPK]
        z = top[0] * 0.0
        for c in top[:PEER_TOPK]:
            z = z + jnp.exp(c - (m1 + m2))
        th = tau - s1
        e1 = jnp.exp(s1 - m1) / z
        e2 = jnp.exp(s2 - m2)
        for lt in range(nlt):
            ls = slice(lt * LANES, (lt + 1) * LANES)
            th_ref[h, lt] = th[:, ls]
            e1_ref[h, lt] = e1[:, ls]
            s2_ref[h, lt] = s2[:, ls]
            e2_ref[h, lt] = e2[:, ls]


def _route(u2, wpq, sk, tm):
    n, d = 2 * u2.shape[0], u2.shape[1]
    heads, _, nkeys, kd = sk.shape
    assert nkeys == LANES and kd == LANES
    nt = n // LANES
    nlt = tm // LANES
    k1 = PEER_TOPK + 1
    ncand = sum(k1 // (i + 1) for i in range(k1))
    ncand_pad = -(-ncand // SUBLANES) * SUBLANES
    list_rows = -(-k1 // SUBLANES) * SUBLANES
    tile = lambda: pl.BlockSpec((heads, nlt, nkeys, LANES), lambda i: (0, i, 0, 0))
    shp = jax.ShapeDtypeStruct((heads, nt, nkeys, LANES), F32)
    return pl.pallas_call(
        functools.partial(_route_kernel, heads=heads),
        grid=(n // tm,),
        in_specs=[
            pl.BlockSpec((tm // 2, d), lambda i: (i, 0)),
            pl.BlockSpec(wpq.shape, lambda i: (0, 0)),
            pl.BlockSpec(sk.shape, lambda i: (0, 0, 0, 0)),
        ],
        out_specs=[tile(), tile(), tile(), tile()],
        out_shape=[shp, shp, shp, shp],
        scratch_shapes=[
            pltpu.VMEM((list_rows, tm), F32),
            pltpu.VMEM((list_rows, tm), F32),
            pltpu.VMEM((ncand_pad, tm), F32),
        ],
        compiler_params=_cparams(("arbitrary",)),
        name="route",
    )(u2, wpq, sk)


def _gelu_times(x, g):
    c = math.sqrt(2.0 / math.pi)
    t = jnp.tanh(x * (c + (c * 0.044715) * (x * x)))
    return (x * g) * (0.5 * t + 0.5)


def _peer_kernel(u_ref, ut_ref, vt_ref, th_ref, e1_ref, s2_ref, e2_ref,
                 o_ref, act_ref, hk_ref, *, heads, na):
    tm = 2 * u_ref.shape[0]
    nlt = tm // LANES

    @pl.when(pl.program_id(1) == 0)
    def _():
        o_ref[...] = jnp.zeros_like(o_ref)

    act_ref[...] = lax.dot_general(ut_ref[...], pltpu.bitcast(u_ref[...], BF16), _NT,
                                   preferred_element_type=F32)

    def gate_rows(a, carry):
        r0 = pl.multiple_of(a * LANES, LANES)
        for lt in range(nlt):
            ls = slice(lt * LANES, (lt + 1) * LANES)
            g = jnp.zeros((LANES, LANES), F32)
            for h in range(heads):
                th = th_ref[h, lt, pl.ds(a, 1), :]
                e1 = e1_ref[h, lt, pl.ds(a, 1), :]
                g = g + jnp.where(s2_ref[h, lt] >= th, e1 * e2_ref[h, lt], 0.0)
            hk_ref[pl.ds(r0, LANES), ls] = _gelu_times(act_ref[pl.ds(r0, LANES), ls], g).astype(BF16)
        return carry

    lax.fori_loop(0, na, gate_rows, 0)
    o_ref[...] += lax.dot_general(hk_ref[...], vt_ref[...], _TN, preferred_element_type=F32)


def _peer(u2, ut, vt, th, e1, s2, e2, tm, te):
    n, d = 2 * u2.shape[0], u2.shape[1]
    ne = vt.shape[0]
    heads = th.shape[0]
    nlt = tm // LANES
    na = te // LANES
    assert na == SUBLANES
    tile = lambda rows: pl.BlockSpec((heads, nlt, rows, LANES), lambda i, j: (0, i, 0, 0))
    arow = lambda: pl.BlockSpec((heads, nlt, na, LANES), lambda i, j: (0, i, j, 0))
    return pl.pallas_call(
        functools.partial(_peer_kernel, heads=heads, na=na),
        grid=(n // tm, ne // te),
        in_specs=[
            pl.BlockSpec((tm // 2, d), lambda i, j: (i, 0)),
            pl.BlockSpec((te, d), lambda i, j: (j, 0)),
            pl.BlockSpec((te, d), lambda i, j: (j, 0)),
            arow(), arow(), tile(LANES), tile(LANES),
        ],
        out_specs=pl.BlockSpec((tm, d), lambda i, j: (i, 0)),
        out_shape=jax.ShapeDtypeStruct((n, d), F32),
        scratch_shapes=[pltpu.VMEM((te, tm), F32), pltpu.VMEM((te, tm), BF16)],
        compiler_params=_cparams(("arbitrary", "arbitrary")),
        name="peer",
    )(u2, ut, vt, th, e1, s2, e2)


def _final_kernel(x1_ref, y_ref, mod_ref, g_ref, b_ref, o_ref, *, alpha):
    m = mod_ref[0]
    o_ref[...] = _ln(alpha * x1_ref[...] + (1.0 + m[5:6]) * y_ref[...]) * g_ref[...] + b_ref[...]


def _final(x1, y, mod, g, b, seq, alpha, tm):
    n, d = x1.shape
    row = lambda: pl.BlockSpec((tm, d), lambda i: (i, 0))
    full = lambda: pl.BlockSpec((1, d), lambda i: (0, 0))
    return pl.pallas_call(
        functools.partial(_final_kernel, alpha=alpha),
        grid=(n // tm,),
        in_specs=[row(), row(), pl.BlockSpec((1, 6, d), lambda i: (i * tm // seq, 0, 0)),
                  full(), full()],
        out_specs=row(),
        out_shape=jax.ShapeDtypeStruct((n, d), F32),
        compiler_params=_cparams(("arbitrary",)),
        name="final",
    )(x1, y, mod, g, b)


def _tile(total, want):
    t = min(total, want)
    while total % t:
        t //= 2
    return t


def _freq_row(rot, period):
    half = rot // 2
    inv = ROPE_THETA ** (-jnp.arange(half, dtype=F32) / half)
    lane = np.arange(LANES) % period
    return jnp.where(lane < rot, jnp.tile(inv, LANES // half), 0.0)[None, :]


def kernel(x, c, positions, w_ada, b_ada, w_in, conv_w, conv_b, w_rg_a, b_rg_a, w_rg_x, b_rg_x,
           lru_lambda, k_idx_g, w_out, ln1_g, ln1_b, w_pq, peer_sub_keys, peer_u, peer_v,
           ln2_g, ln2_b):
    batch, seq, d = x.shape
    depth = w_ada.shape[0]
    n = batch * seq
    d_lru = conv_w.shape[-1]
    d_att = w_out.shape[1] - d_lru
    d_in = w_in.shape[-1]
    ni = (d_in - 2 * d_lru - 3 * d_att - IDX_DIM) // (IDX_DIM + 1)
    assert 2 * d_lru + 3 * d_att + ni * IDX_DIM + IDX_DIM + ni == d_in
    alpha = (2.0 * depth) ** 0.25

    nout = -(-d_in // (2 * LANES)) * (2 * LANES)
    tn_in = _tile(nout, 1280) if nout % 1280 == 0 else 2 * LANES
    pos = positions.reshape(n, 1).astype(I32)
    fa = _freq_row(ROT_ATT, HEAD_DIM)
    fi = _freq_row(ROT_IDX, IDX_DIM)
    c_pad = jnp.pad(c, ((0, SUBLANES - batch), (0, 0)))

    xf = x.reshape(n, d)
    for l in range(depth):
        mod = _ada(c_pad, w_ada[l], b_ada[l][None, :])[:batch].reshape(batch, 6, d)
        w_in_p = jnp.pad(w_in[l], ((0, 0), (0, nout - d_in))).astype(BF16)
        proj = _inproj(xf, mod, w_in_p, seq, _tile(seq, 1024), tn_in)
        y_lru = _lru(proj, conv_w[l], conv_b[l][None, :], w_rg_a[l].astype(BF16), b_rg_a[l][None, :],
                     w_rg_x[l].astype(BF16), b_rg_x[l][None, :], lru_lambda[l][None, :],
                     batch, seq, _tile(seq, 256))
        g_pad = jnp.pad(k_idx_g[l], (0, LANES - IDX_DIM))[None, :]
        q, k, v, qi, kd, w = _dsaprep(proj, pos, g_pad, fa, fi, d_lru, d_att, ni, _tile(seq, 512))
        y_att = _dsa(q, k, v, qi, kd, w, batch, seq, _tile(seq, 2 * Q_BLOCK), _tile(seq, 512))
        wo = w_out[l].astype(BF16)
        x1, u2 = _outproj(y_lru, y_att, xf, mod, wo[:d_lru], wo[d_lru:], ln1_g[l][None, :],
                          ln1_b[l][None, :], seq, alpha, _tile(seq, 256))
        th, e1, s2, e2 = _route(u2, w_pq[l].astype(BF16), peer_sub_keys[l].astype(BF16),
                                _tile(seq, 256))
        y = _peer(u2, peer_u[l].astype(BF16), peer_v[l].astype(BF16), th, e1, s2, e2,
                  _tile(seq, 512), SUBLANES * LANES)
        xf = _final(x1, y, mod, ln2_g[l][None, :], ln2_b[l][None, :], seq, alpha, _tile(seq, 512))
    return xf.reshape(batch, seq, d)
```

```python
import functools
import math

import jax
import jax.numpy as jnp
import numpy as np
from jax import lax
from jax.experimental import pallas as pl
from jax.experimental.pallas import tpu as pltpu

F32 = jnp.float32
BF16 = jnp.bfloat16
I32 = jnp.int32

LANES = 128
SUBLANES = 8
VMEM_LIMIT = 60 * 1024 * 1024

LN_EPS = 1e-5
CHUNK = 64
CHUNK_SHIFT = 6
assert 1 << CHUNK_SHIFT == CHUNK
CONV_W = 4
LRU_C = 8.0
HEAD_DIM = 128
IDX_DIM = 64
TOPK_MAX = 256
Q_BLOCK = 128
ROPE_THETA = 500000.0
ROT_ATT = HEAD_DIM // 4
ROT_IDX = IDX_DIM // 4
PEER_TOPK = 16
NEG_BIG = -1e30

MAX_BISECT = 300

_NT = (((1,), (1,)), ((), ()))
_TN = (((0,), (0,)), ((), ()))


def _cparams(sem):
    return pltpu.CompilerParams(dimension_semantics=sem, vmem_limit_bytes=VMEM_LIMIT)


def _ln(x):
    mu = jnp.mean(x, axis=-1, keepdims=True)
    xc = x - mu
    var = jnp.mean(xc * xc, axis=-1, keepdims=True)
    return xc * lax.rsqrt(var + LN_EPS)


def _gelu(x):
    return 0.5 * x * (1.0 + jnp.tanh(math.sqrt(2.0 / math.pi) * (x + 0.044715 * (x * x * x))))


def _ada_kernel(c_ref, w_ref, b_ref, o_ref):
    c = c_ref[...]
    s = c * jax.nn.sigmoid(c)
    o_ref[...] = jnp.dot(s, w_ref[...], preferred_element_type=F32) + b_ref[...]


def _ada(c_pad, w, b, tn=512):
    rows, d = c_pad.shape
    n6 = w.shape[1]
    return pl.pallas_call(
        _ada_kernel,
        grid=(n6 // tn,),
        in_specs=[
            pl.BlockSpec((rows, d), lambda j: (0, 0)),
            pl.BlockSpec((d, tn), lambda j: (0, j)),
            pl.BlockSpec((1, tn), lambda j: (0, j)),
        ],
        out_specs=pl.BlockSpec((rows, tn), lambda j: (0, j)),
        out_shape=jax.ShapeDtypeStruct((rows, n6), F32),
        compiler_params=_cparams(("arbitrary",)),
        name="ada",
    )(c_pad, w, b)


def _inproj_kernel(x_ref, mod_ref, w_ref, o_ref, u_ref):
    @pl.when(pl.program_id(1) == 0)
    def _():
        m = mod_ref[0]
        u = _ln(x_ref[...]) * (1.0 + m[1:2]) + m[0:1]
        u_ref[...] = u.astype(BF16)

    o_ref[...] = jnp.dot(u_ref[...], w_ref[...], preferred_element_type=F32)


def _inproj(x, mod, w, seq, tm, tn):
    n, d = x.shape
    nout = w.shape[1]
    return pl.pallas_call(
        _inproj_kernel,
        grid=(n // tm, nout // tn),
        in_specs=[
            pl.BlockSpec((tm, d), lambda i, j: (i, 0)),
            pl.BlockSpec((1, 6, d), lambda i, j: (i * tm // seq, 0, 0)),
            pl.BlockSpec((d, tn), lambda i, j: (0, j)),
        ],
        out_specs=pl.BlockSpec((tm, tn), lambda i, j: (i, j)),
        out_shape=jax.ShapeDtypeStruct((n, nout), F32),
        scratch_shapes=[pltpu.VMEM((tm, d), BF16)],
        compiler_params=_cparams(("arbitrary", "arbitrary")),
        name="inproj",
    )(x, mod, w)


def _lru_kernel(xl_ref, gl_ref, cw_ref, cb_ref, wa_ref, ba_ref, wx_ref, bx_ref, lam_ref,
                o_ref, xe_ref, a_ref, b_ref, h_ref, *, nblk):
    tc, c = xl_ref.shape

    @pl.when(pl.program_id(1) == 0)
    def _():
        xe_ref[0:SUBLANES, :] = jnp.zeros((SUBLANES, c), F32)
        h_ref[...] = jnp.zeros_like(h_ref)

    x = xl_ref[...]
    xe_ref[SUBLANES:SUBLANES + tc, :] = x
    cw = cw_ref[...]
    xc = (x * cw[3:4]
          + xe_ref[pl.ds(SUBLANES - 1, tc), :] * cw[2:3]
          + xe_ref[pl.ds(SUBLANES - 2, tc), :] * cw[1:2]
          + xe_ref[pl.ds(SUBLANES - 3, tc), :] * cw[0:1]
          + cb_ref[...])
    xe_ref[0:SUBLANES, :] = x[tc - SUBLANES:tc, :]

    lam = lam_ref[...]
    z = -lam
    softplus = jnp.maximum(z, 0.0) + jnp.log1p(jnp.exp(-jnp.abs(z)))
    for nb in range(nblk):
        sl = slice(nb * LANES, (nb + 1) * LANES)
        xb = xc[:, sl]
        xb16 = xb.astype(BF16)
        r = jax.nn.sigmoid(jnp.dot(xb16, wa_ref[nb], preferred_element_type=F32) + ba_ref[:, sl])
        i = jax.nn.sigmoid(jnp.dot(xb16, wx_ref[nb], preferred_element_type=F32) + bx_ref[:, sl])
        log_a = (-LRU_C) * r * softplus[:, sl]
        a = jnp.exp(log_a)
        one_m_a2 = -jnp.tanh(log_a) * (a * a + 1.0)
        a_ref[:, sl] = a
        b_ref[:, sl] = jnp.sqrt(one_m_a2) * (i * xb)

    row = lax.broadcasted_iota(I32, (SUBLANES, c), 0)

    def body(g, h):
        r0 = pl.multiple_of(g * SUBLANES, SUBLANES)
        a = a_ref[pl.ds(r0, SUBLANES), :]
        b = b_ref[pl.ds(r0, SUBLANES), :]
        for d in (1, 2, 4):
            keep = row >= d
            a_s = pltpu.roll(a, d, 0)
            b_s = pltpu.roll(b, d, 0)
            b = jnp.where(keep, a * b_s + b, b)
            a = jnp.where(keep, a * a_s, a)
        hh = a * h + b
        o_ref[pl.ds(r0, SUBLANES), :] = (hh * _gelu(gl_ref[pl.ds(r0, SUBLANES), :])).astype(o_ref.dtype)
        return hh[SUBLANES - 1:SUBLANES, :]

    h_ref[0:1, :] = lax.fori_loop(0, tc // SUBLANES, body, h_ref[0:1, :])


def _lru(proj, cw, cb, wa, ba, wx, bx, lam, batch, seq, tc):
    n = proj.shape[0]
    c = cw.shape[1]
    nblk = wa.shape[0]
    nt = seq // tc
    full = lambda shape: pl.BlockSpec(shape, lambda b, t: (0,) * len(shape))
    return pl.pallas_call(
        functools.partial(_lru_kernel, nblk=nblk),
        grid=(batch, nt),
        in_specs=[
            pl.BlockSpec((tc, c), lambda b, t: (b * nt + t, 0)),
            pl.BlockSpec((tc, c), lambda b, t: (b * nt + t, 1)),
            full((CONV_W, c)), full((1, c)),
            full((nblk, LANES, LANES)), full((1, c)),
            full((nblk, LANES, LANES)), full((1, c)),
            full((1, c)),
        ],
        out_specs=pl.BlockSpec((tc, c), lambda b, t: (b * nt + t, 0)),
        out_shape=jax.ShapeDtypeStruct((n, c), BF16),
        scratch_shapes=[
            pltpu.VMEM((tc + SUBLANES, c), F32),
            pltpu.VMEM((tc, c), F32),
            pltpu.VMEM((tc, c), F32),
            pltpu.VMEM((SUBLANES, c), F32),
        ],
        compiler_params=_cparams(("arbitrary", "arbitrary")),
        name="lru",
    )(proj, proj, cw, cb, wa, ba, wx, bx, lam)


def _rope_tables(ang, lane_in_head, half):
    cos = jnp.cos(ang)
    sin = jnp.sin(ang)
    c = jnp.where(lane_in_head < 2 * half, cos, 1.0)
    s1 = jnp.where(lane_in_head < half, -sin, 0.0)
    s2 = jnp.where((lane_in_head >= half) & (lane_in_head < 2 * half), sin, 0.0)
    return c, s1, s2


def _rope(x, tabs, half):
    c, s1, s2 = tabs
    return x * c + pltpu.roll(x, LANES - half, 1) * s1 + pltpu.roll(x, half, 1) * s2


def _dsaprep_kernel(q_ref, k_ref, v_ref, qi_ref, kw_ref, pos_ref, g_ref, fa_ref, fi_ref,
                    qo_ref, ko_ref, vo_ref, qio_ref, kdo_ref, wo_ref, *, nh, npair, wscale):
    tp = q_ref.shape[0]
    pos = pos_ref[...].astype(F32)
    lane = lax.broadcasted_iota(I32, (tp, LANES), 1)
    tab_a = _rope_tables(pos * fa_ref[...], lane, ROT_ATT // 2)
    tab_i = _rope_tables(pos * fi_ref[...], lane % IDX_DIM, ROT_IDX // 2)
    for h in range(nh):
        sl = slice(h * LANES, (h + 1) * LANES)
        qo_ref[:, sl] = _rope(q_ref[:, sl], tab_a, ROT_ATT // 2).astype(BF16)
        ko_ref[:, sl] = _rope(k_ref[:, sl], tab_a, ROT_ATT // 2).astype(BF16)
    vo_ref[...] = v_ref[...].astype(BF16)
    first = lane < IDX_DIM
    for p in range(npair):
        r = _rope(qi_ref[:, p * LANES:(p + 1) * LANES], tab_i, ROT_IDX // 2)
        qio_ref[:, (2 * p) * LANES:(2 * p + 1) * LANES] = jnp.where(first, r, 0.0).astype(BF16)
        qio_ref[:, (2 * p + 1) * LANES:(2 * p + 2) * LANES] = jnp.where(first, 0.0, r).astype(BF16)
    kw = kw_ref[...]
    mu = jnp.sum(jnp.where(first, kw, 0.0), axis=1, keepdims=True) * (1.0 / IDX_DIM)
    kc = jnp.where(first, kw - mu, 0.0)
    var = jnp.sum(kc * kc, axis=1, keepdims=True) * (1.0 / IDX_DIM)
    kn = kc * lax.rsqrt(var + LN_EPS) * g_ref[...]
    kr = _rope(kn, tab_i, ROT_IDX // 2)
    kr = jnp.where(first, kr, 0.0)
    kdo_ref[...] = (kr + pltpu.roll(kr, IDX_DIM, 1)).astype(BF16)
    wo_ref[...] = kw * wscale


def _dsaprep(proj, pos, g_pad, fa, fi, d_lru, d_att, ni, tp):
    n = proj.shape[0]
    nh = d_att // HEAD_DIM
    npair = ni * IDX_DIM // LANES
    assert d_lru % d_att == 0 or d_att % d_lru == 0
    q0 = 2 * d_lru // d_att
    kw_blk = (2 * d_lru + 3 * d_att + ni * IDX_DIM) // LANES
    assert (ni * IDX_DIM) % d_att == 0 or ni * IDX_DIM == d_att
    wide = lambda j: pl.BlockSpec((tp, d_att), lambda i: (i, j))
    row1 = lambda: pl.BlockSpec((1, LANES), lambda i: (0, 0))
    out_w = lambda: pl.BlockSpec((tp, d_att), lambda i: (i, 0))
    out_n = lambda: pl.BlockSpec((tp, LANES), lambda i: (i, 0))
    wscale = float(ni) ** -0.5 * float(IDX_DIM) ** -0.5
    return pl.pallas_call(
        functools.partial(_dsaprep_kernel, nh=nh, npair=npair, wscale=wscale),
        grid=(n // tp,),
        in_specs=[
            wide(q0), wide(q0 + 1), wide(q0 + 2),
            pl.BlockSpec((tp, ni * IDX_DIM), lambda i: (i, (2 * d_lru + 3 * d_att) // (ni * IDX_DIM))),
            pl.BlockSpec((tp, LANES), lambda i: (i, kw_blk)),
            pl.BlockSpec((tp, 1), lambda i: (i, 0)),
            row1(), row1(), row1(),
        ],
        out_specs=[out_w(), out_w(), out_w(),
                   pl.BlockSpec((tp, ni * LANES), lambda i: (i, 0)), out_n(), out_n()],
        out_shape=[
            jax.ShapeDtypeStruct((n, d_att), BF16),
            jax.ShapeDtypeStruct((n, d_att), BF16),
            jax.ShapeDtypeStruct((n, d_att), BF16),
            jax.ShapeDtypeStruct((n, ni * LANES), BF16),
            jax.ShapeDtypeStruct((n, LANES), BF16),
            jax.ShapeDtypeStruct((n, LANES), F32),
        ],
        compiler_params=_cparams(("arbitrary",)),
        name="dsaprep",
    )(proj, proj, proj, proj, proj, pos, g_pad, fa, fi)


def _dsa_kernel(q_ref, qi_ref, w_ref, k_ref, v_ref, kd_ref, o_ref, sc_ref, bias_ref, wb_ref,
                m_ref, l_ref, acc_ref, *, tq, tk, nh, ni, topk, att_scale):
    qb = pl.program_id(1)
    ntile = ((qb + 1) * tq + tk - 1) // tk
    nsub = tk // LANES
    topk_f = float(topk)

    w = w_ref[...]
    for h in range(ni):
        wb_ref[h] = jnp.broadcast_to(w[:, IDX_DIM + h:IDX_DIM + h + 1], (tq, LANES))
    row = lax.broadcasted_iota(I32, (tq, tk), 0)
    lane = lax.broadcasted_iota(I32, (tq, tk), 1)
    bound = (((qb * tq + row) >> CHUNK_SHIFT) + 1) << CHUNK_SHIFT

    n_adm = bound[:, 0:1].astype(F32)

    def score_tile(j, carry):
        mx, mn = carry
        k0 = pl.multiple_of(j * tk, tk)
        kd = kd_ref[pl.ds(k0, tk), :]
        acc = [jnp.zeros((tq, LANES), F32) for _ in range(nsub)]
        for h in range(ni):
            qh = qi_ref[:, h * LANES:(h + 1) * LANES]
            logit = lax.dot_general(qh, kd, _NT, preferred_element_type=F32)
            wh = wb_ref[h]
            for s in range(nsub):
                acc[s] = acc[s] + wh * jnp.maximum(logit[:, s * LANES:(s + 1) * LANES], 0.0)
        score = jnp.concatenate(acc, axis=1)
        adm = (k0 + lane) < bound
        sc_ref[j] = jnp.where(adm, score, -jnp.inf)
        mx = jnp.maximum(mx, jnp.max(jnp.where(adm, score, -jnp.inf), axis=1, keepdims=True))
        mn = jnp.minimum(mn, jnp.min(jnp.where(adm, score, jnp.inf), axis=1, keepdims=True))
        return mx, mn

    mx, mn = lax.fori_loop(0, ntile, score_tile,
                           (jnp.full((tq, 1), -jnp.inf, F32), jnp.full((tq, 1), jnp.inf, F32)))

    def count_ge(mid):
        def body(j, acc):
            ge = jnp.where(sc_ref[j] >= mid, 1.0, 0.0)
            for s in range(nsub):
                acc = acc + ge[:, s * LANES:(s + 1) * LANES]
            return acc
        acc = lax.fori_loop(0, ntile, body, jnp.zeros((tq, LANES), F32))
        return jnp.sum(acc, axis=1, keepdims=True)

    small = n_adm <= topk_f

    def pending(resolved):
        return jnp.max(jnp.where(resolved, 0.0, 1.0))

    def bisect_cond(c):
        return (c[0] < MAX_BISECT) & (c[4] > 0.0)

    def bisect(c):
        it, lo, hi, cnt_lo, _ = c
        mid = 0.5 * lo + 0.5 * hi
        cnt = count_ge(mid)
        ge = cnt >= topk_f
        adjacent = (mid <= lo) | (mid >= hi)
        lo = jnp.where(ge, mid, lo)
        cnt_lo = jnp.where(ge, cnt, cnt_lo)
        hi = jnp.where(ge, hi, mid)
        return it + 1, lo, hi, cnt_lo, pending((cnt_lo == topk_f) | small | adjacent)

    _, thr, _, _, _ = lax.while_loop(
        bisect_cond, bisect, (jnp.int32(0), mn, mx, n_adm, pending((n_adm == topk_f) | small)))
    thr = jnp.where(small, -jnp.inf, thr)

    def bias_tile(j, carry):
        k0 = pl.multiple_of(j * tk, tk)
        sel = (sc_ref[j] >= thr) & ((k0 + lane) < bound)
        bias_ref[j] = jnp.where(sel, 0.0, NEG_BIG)
        return carry

    lax.fori_loop(0, ntile, bias_tile, 0)

    for h in range(nh):
        m_ref[h] = jnp.full((tq, LANES), NEG_BIG, F32)
        l_ref[h] = jnp.zeros((tq, LANES), F32)
        acc_ref[h] = jnp.zeros((tq, HEAD_DIM), F32)

    def att_tile(j, carry):
        k0 = pl.multiple_of(j * tk, tk)
        bias = bias_ref[j]
        heads_sl = [slice(h * HEAD_DIM, (h + 1) * HEAD_DIM) for h in range(nh)]
        qk = [lax.dot_general(q_ref[:, sl], k_ref[pl.ds(k0, tk), sl], _NT, preferred_element_type=F32)
              for sl in heads_sl]
        for h, sl in enumerate(heads_sl):
            vt = v_ref[pl.ds(k0, tk), sl]
            s = qk[h] * att_scale + bias
            m_old = m_ref[h]
            m_new = jnp.maximum(m_old, jnp.max(s, axis=1, keepdims=True))
            alpha = jnp.exp(m_old - m_new)
            p = jnp.exp(s - jnp.concatenate([m_new] * nsub, axis=1))
            l_ref[h] = alpha * l_ref[h] + jnp.sum(p, axis=1, keepdims=True)
            acc_ref[h] = alpha * acc_ref[h] + jnp.dot(p.astype(BF16), vt, preferred_element_type=F32)
            m_ref[h] = m_new
        return carry

    lax.fori_loop(0, ntile, att_tile, 0)
    for h in range(nh):
        sl = slice(h * HEAD_DIM, (h + 1) * HEAD_DIM)
        o_ref[:, sl] = (acc_ref[h] / l_ref[h]).astype(o_ref.dtype)


def _dsa(q, k, v, qi, kd, w, batch, seq, tq, tk):
    n, d_att = q.shape
    nh = d_att // HEAD_DIM
    ni = qi.shape[1] // LANES
    topk = min(TOPK_MAX, seq // 4)
    nq = seq // tq
    assert tq % CHUNK == 0 and seq % tq == 0 and seq % tk == 0 and tk % CHUNK == 0 and tk >= topk
    qspec = lambda width: pl.BlockSpec((tq, width), lambda b, i: (b * nq + i, 0))
    kspec = lambda width: pl.BlockSpec((seq, width), lambda b, i: (b, 0), pipeline_mode=pl.Buffered(1))
    return pl.pallas_call(
        functools.partial(_dsa_kernel, tq=tq, tk=tk, nh=nh, ni=ni, topk=topk,
                          att_scale=float(HEAD_DIM) ** -0.5),
        grid=(batch, nq),
        in_specs=[qspec(d_att), qspec(ni * LANES), qspec(LANES),
                  kspec(d_att), kspec(d_att), kspec(LANES)],
        out_specs=qspec(d_att),
        out_shape=jax.ShapeDtypeStruct((n, d_att), BF16),
        scratch_shapes=[
            pltpu.VMEM((seq // tk, tq, tk), F32),
            pltpu.VMEM((seq // tk, tq, tk), F32),
            pltpu.VMEM((ni, tq, LANES), F32),
            pltpu.VMEM((nh, tq, LANES), F32),
            pltpu.VMEM((nh, tq, LANES), F32),
            pltpu.VMEM((nh, tq, HEAD_DIM), F32),
        ],
        compiler_params=_cparams(("arbitrary", "arbitrary")),
        name="dsa",
    )(q, qi, w, k, v, kd)


def _outproj_kernel(yl_ref, ya_ref, x_ref, mod_ref, wl_ref, wa_ref, g_ref, b_ref,
                    x1_ref, u2_ref, *, alpha):
    m = mod_ref[0]
    mix = (jnp.dot(yl_ref[...], wl_ref[...], preferred_element_type=F32)
           + jnp.dot(ya_ref[...], wa_ref[...], preferred_element_type=F32))
    x1 = _ln(alpha * x_ref[...] + (1.0 + m[2:3]) * mix) * g_ref[...] + b_ref[...]
    x1_ref[...] = x1
    u2_ref[...] = pltpu.bitcast((_ln(x1) * (1.0 + m[4:5]) + m[3:4]).astype(BF16), I32)


def _outproj(yl, ya, x, mod, wl, wa, g, b, seq, alpha, tm):
    n, d = x.shape
    row = lambda width: pl.BlockSpec((tm, width), lambda i: (i, 0))
    full = lambda shape: pl.BlockSpec(shape, lambda i: (0,) * len(shape))
    return pl.pallas_call(
        functools.partial(_outproj_kernel, alpha=alpha),
        grid=(n // tm,),
        in_specs=[row(yl.shape[1]), row(ya.shape[1]), row(d),
                  pl.BlockSpec((1, 6, d), lambda i: (i * tm // seq, 0, 0)),
                  full(wl.shape), full(wa.shape), full((1, d)), full((1, d))],
        out_specs=[row(d), pl.BlockSpec((tm // 2, d), lambda i: (i, 0))],
        out_shape=[jax.ShapeDtypeStruct((n, d), F32), jax.ShapeDtypeStruct((n // 2, d), I32)],
        compiler_params=_cparams(("arbitrary",)),
        name="outproj",
    )(yl, ya, x, mod, wl, wa, g, b)


def _topk_desc(x, k):
    out = []
    for _ in range(k):
        m = jnp.max(x, axis=0, keepdims=True)
        out.append(m)
        x = jnp.where(x == m, -jnp.inf, x)
    return out


def _route_kernel(u_ref, wpq_ref, sk_ref, th_ref, e1_ref, s2_ref, e2_ref,
                  v1_ref, v2_ref, cand_ref, *, heads):
    tm = 2 * u_ref.shape[0]
    nlt = tm // LANES
    k1 = PEER_TOPK + 1
    qp = jnp.dot(pltpu.bitcast(u_ref[...], BF16), wpq_ref[...], preferred_element_type=F32).astype(BF16)
    cand_ref[...] = jnp.full(cand_ref.shape, -jnp.inf, F32)
    for h in range(heads):
        st = []
        for p in range(2):
            col = (2 * h + p) * LANES
            st.append(lax.dot_general(sk_ref[h, p], qp[:, col:col + LANES], _NT,
                                      preferred_element_type=F32))
        s1, s2 = st
        for k, m in enumerate(_topk_desc(s1, k1)):
            v1_ref[k:k + 1, :] = m
        for k, m in enumerate(_topk_desc(s2, k1)):
            v2_ref[k:k + 1, :] = m
        off = 0
        for i in range(k1):
            nj = k1 // (i + 1)
            cand_ref[off:off + nj, :] = v1_ref[i:i + 1, :] + v2_ref[0:nj, :]
            off += nj
        m1 = v1_ref[0:1, :]
        m2 = v2_ref[0:1, :]
        top = _topk_desc(cand_ref[...], k1)
        tau = 0.5 * top[PEER_TOPK - 1] + 0.5 * top[PEER_TOPK]
        z = top[0] * 0.0
        for c in top[:PEER_TOPK]:
            z = z + jnp.exp(c - (m1 + m2))
        th = tau - s1
        e1 = jnp.exp(s1 - m1) / z
        e2 = jnp.exp(s2 - m2)
        for lt in range(nlt):
            ls = slice(lt * LANES, (lt + 1) * LANES)
            th_ref[h, lt] = th[:, ls]
            e1_ref[h, lt] = e1[:, ls]
            s2_ref[h, lt] = s2[:, ls]
            e2_ref[h, lt] = e2[:, ls]


def _route(u2, wpq, sk, tm):
    n, d = 2 * u2.shape[0], u2.shape[1]
    heads, _, nkeys, kd = sk.shape
    assert nkeys == LANES and kd == LANES
    nt = n // LANES
    nlt = tm // LANES
    k1 = PEER_TOPK + 1
    ncand = sum(k1 // (i + 1) for i in range(k1))
    ncand_pad = -(-ncand // SUBLANES) * SUBLANES
    list_rows = -(-k1 // SUBLANES) * SUBLANES
    tile = lambda: pl.BlockSpec((heads, nlt, nkeys, LANES), lambda i: (0, i, 0, 0))
    shp = jax.ShapeDtypeStruct((heads, nt, nkeys, LANES), F32)
    return pl.pallas_call(
        functools.partial(_route_kernel, heads=heads),
        grid=(n // tm,),
        in_specs=[
            pl.BlockSpec((tm // 2, d), lambda i: (i, 0)),
            pl.BlockSpec(wpq.shape, lambda i: (0, 0)),
            pl.BlockSpec(sk.shape, lambda i: (0, 0, 0, 0)),
        ],
        out_specs=[tile(), tile(), tile(), tile()],
        out_shape=[shp, shp, shp, shp],
        scratch_shapes=[
            pltpu.VMEM((list_rows, tm), F32),
            pltpu.VMEM((list_rows, tm), F32),
            pltpu.VMEM((ncand_pad, tm), F32),
        ],
        compiler_params=_cparams(("arbitrary",)),
        name="route",
    )(u2, wpq, sk)


def _gelu_times(x, g):
    c = math.sqrt(2.0 / math.pi)
    t = jnp.tanh(x * (c + (c * 0.044715) * (x * x)))
    return (x * g) * (0.5 * t + 0.5)


def _peer_kernel(u_ref, ut_ref, vt_ref, th_ref, e1_ref, s2_ref, e2_ref,
                 o_ref, act_ref, hk_ref, *, heads, na):
    tm = 2 * u_ref.shape[0]
    nlt = tm // LANES

    @pl.when(pl.program_id(1) == 0)
    def _():
        o_ref[...] = jnp.zeros_like(o_ref)

    act_ref[...] = lax.dot_general(ut_ref[...], pltpu.bitcast(u_ref[...], BF16), _NT,
                                   preferred_element_type=F32)

    def gate_rows(a, carry):
        r0 = pl.multiple_of(a * LANES, LANES)
        for lt in range(nlt):
            ls = slice(lt * LANES, (lt + 1) * LANES)
            g = jnp.zeros((LANES, LANES), F32)
            for h in range(heads):
                th = th_ref[h, lt, pl.ds(a, 1), :]
                e1 = e1_ref[h, lt, pl.ds(a, 1), :]
                g = g + jnp.where(s2_ref[h, lt] >= th, e1 * e2_ref[h, lt], 0.0)
            hk_ref[pl.ds(r0, LANES), ls] = _gelu_times(act_ref[pl.ds(r0, LANES), ls], g).astype(BF16)
        return carry

    lax.fori_loop(0, na, gate_rows, 0)
    o_ref[...] += lax.dot_general(hk_ref[...], vt_ref[...], _TN, preferred_element_type=F32)


def _peer(u2, ut, vt, th, e1, s2, e2, tm, te):
    n, d = 2 * u2.shape[0], u2.shape[1]
    ne = vt.shape[0]
    heads = th.shape[0]
    nlt = tm // LANES
    na = te // LANES
    assert na == SUBLANES
    tile = lambda rows: pl.BlockSpec((heads, nlt, rows, LANES), lambda i, j: (0, i, 0, 0),
                                     pipeline_mode=pl.Buffered(1))
    arow = lambda: pl.BlockSpec((heads, nlt, na, LANES), lambda i, j: (0, i, j, 0))
    return pl.pallas_call(
        functools.partial(_peer_kernel, heads=heads, na=na),
        grid=(n // tm, ne // te),
        in_specs=[
            pl.BlockSpec((tm // 2, d), lambda i, j: (i, 0)),
            pl.BlockSpec((te, d), lambda i, j: (j, 0)),
            pl.BlockSpec((te, d), lambda i, j: (j, 0)),
            arow(), arow(), tile(LANES), tile(LANES),
        ],
        out_specs=pl.BlockSpec((tm, d), lambda i, j: (i, 0)),
        out_shape=jax.ShapeDtypeStruct((n, d), F32),
        scratch_shapes=[pltpu.VMEM((te, tm), F32), pltpu.VMEM((te, tm), BF16)],
        compiler_params=_cparams(("arbitrary", "arbitrary")),
        name="peer",
    )(u2, ut, vt, th, e1, s2, e2)


def _final_kernel(x1_ref, y_ref, mod_ref, g_ref, b_ref, o_ref, *, alpha):
    m = mod_ref[0]
    o_ref[...] = _ln(alpha * x1_ref[...] + (1.0 + m[5:6]) * y_ref[...]) * g_ref[...] + b_ref[...]


def _final(x1, y, mod, g, b, seq, alpha, tm):
    n, d = x1.shape
    row = lambda: pl.BlockSpec((tm, d), lambda i: (i, 0))
    full = lambda: pl.BlockSpec((1, d), lambda i: (0, 0))
    return pl.pallas_call(
        functools.partial(_final_kernel, alpha=alpha),
        grid=(n // tm,),
        in_specs=[row(), row(), pl.BlockSpec((1, 6, d), lambda i: (i * tm // seq, 0, 0)),
                  full(), full()],
        out_specs=row(),
        out_shape=jax.ShapeDtypeStruct((n, d), F32),
        compiler_params=_cparams(("arbitrary",)),
        name="final",
    )(x1, y, mod, g, b)


def _tile(total, want):
    t = min(total, want)
    while total % t:
        t //= 2
    return t


def _freq_row(rot, period):
    half = rot // 2
    inv = ROPE_THETA ** (-jnp.arange(half, dtype=F32) / half)
    lane = np.arange(LANES) % period
    return jnp.where(lane < rot, jnp.tile(inv, LANES // half), 0.0)[None, :]


def kernel(x, c, positions, w_ada, b_ada, w_in, conv_w, conv_b, w_rg_a, b_rg_a, w_rg_x, b_rg_x,
           lru_lambda, k_idx_g, w_out, ln1_g, ln1_b, w_pq, peer_sub_keys, peer_u, peer_v,
           ln2_g, ln2_b):
    batch, seq, d = x.shape
    depth = w_ada.shape[0]
    n = batch * seq
    d_lru = conv_w.shape[-1]
    d_att = w_out.shape[1] - d_lru
    d_in = w_in.shape[-1]
    ni = (d_in - 2 * d_lru - 3 * d_att - IDX_DIM) // (IDX_DIM + 1)
    assert 2 * d_lru + 3 * d_att + ni * IDX_DIM + IDX_DIM + ni == d_in
    alpha = (2.0 * depth) ** 0.25

    nout = -(-d_in // (2 * LANES)) * (2 * LANES)
    tn_in = _tile(nout, 1280) if nout % 1280 == 0 else 2 * LANES
    pos = positions.reshape(n, 1).astype(I32)
    fa = _freq_row(ROT_ATT, HEAD_DIM)
    fi = _freq_row(ROT_IDX, IDX_DIM)
    c_pad = jnp.pad(c, ((0, SUBLANES - batch), (0, 0)))

    xf = x.reshape(n, d)
    for l in range(depth):
        mod = _ada(c_pad, w_ada[l], b_ada[l][None, :])[:batch].reshape(batch, 6, d)
        w_in_p = jnp.pad(w_in[l], ((0, 0), (0, nout - d_in))).astype(BF16)
        proj = _inproj(xf, mod, w_in_p, seq, _tile(seq, 1024), tn_in)
        y_lru = _lru(proj, conv_w[l], conv_b[l][None, :], w_rg_a[l].astype(BF16), b_rg_a[l][None, :],
                     w_rg_x[l].astype(BF16), b_rg_x[l][None, :], lru_lambda[l][None, :],
                     batch, seq, _tile(seq, 256))
        g_pad = jnp.pad(k_idx_g[l], (0, LANES - IDX_DIM))[None, :]
        q, k, v, qi, kd, w = _dsaprep(proj, pos, g_pad, fa, fi, d_lru, d_att, ni, _tile(seq, 512))
        y_att = _dsa(q, k, v, qi, kd, w, batch, seq, _tile(seq, 2 * Q_BLOCK), _tile(seq, 512))
        wo = w_out[l].astype(BF16)
        x1, u2 = _outproj(y_lru, y_att, xf, mod, wo[:d_lru], wo[d_lru:], ln1_g[l][None, :],
                          ln1_b[l][None, :], seq, alpha, _tile(seq, 256))
        th, e1, s2, e2 = _route(u2, w_pq[l].astype(BF16), peer_sub_keys[l].astype(BF16),
                                _tile(seq, 256))
        y = _peer(u2, peer_u[l].astype(BF16), peer_v[l].astype(BF16), th, e1, s2, e2,
                  _tile(seq, 1024), SUBLANES * LANES)
        xf = _final(x1, y, mod, ln2_g[l][None, :], ln2_b[l][None, :], seq, alpha, _tile(seq, 512))
    return xf.reshape(batch, seq, d)
```

```python
import functools
import math

import jax
import jax.numpy as jnp
import numpy as np
from jax import lax
from jax.experimental import pallas as pl
from jax.experimental.pallas import tpu as pltpu

F32 = jnp.float32
BF16 = jnp.bfloat16
I32 = jnp.int32

LANES = 128
SUBLANES = 8
VMEM_LIMIT = 60 * 1024 * 1024

LN_EPS = 1e-5
CHUNK = 64
CHUNK_SHIFT = 6
assert 1 << CHUNK_SHIFT == CHUNK
CONV_W = 4
LRU_C = 8.0
HEAD_DIM = 128
IDX_DIM = 64
TOPK_MAX = 256
Q_BLOCK = 128
ROPE_THETA = 500000.0
ROT_ATT = HEAD_DIM // 4
ROT_IDX = IDX_DIM // 4
PEER_TOPK = 16
NEG_BIG = -1e30

MAX_BISECT = 300

_NT = (((1,), (1,)), ((), ()))
_TN = (((0,), (0,)), ((), ()))


def _cparams(sem):
    return pltpu.CompilerParams(dimension_semantics=sem, vmem_limit_bytes=VMEM_LIMIT)


def _ln(x):
    mu = jnp.mean(x, axis=-1, keepdims=True)
    xc = x - mu
    var = jnp.mean(xc * xc, axis=-1, keepdims=True)
    return xc * lax.rsqrt(var + LN_EPS)


def _gelu(x):
    return 0.5 * x * (1.0 + jnp.tanh(math.sqrt(2.0 / math.pi) * (x + 0.044715 * (x * x * x))))


def _ada_kernel(c_ref, w_ref, b_ref, o_ref):
    c = c_ref[...]
    s = c * jax.nn.sigmoid(c)
    o_ref[...] = jnp.dot(s, w_ref[...], preferred_element_type=F32) + b_ref[...]


def _ada(c_pad, w, b, tn=512):
    rows, d = c_pad.shape
    n6 = w.shape[1]
    return pl.pallas_call(
        _ada_kernel,
        grid=(n6 // tn,),
        in_specs=[
            pl.BlockSpec((rows, d), lambda j: (0, 0)),
            pl.BlockSpec((d, tn), lambda j: (0, j)),
            pl.BlockSpec((1, tn), lambda j: (0, j)),
        ],
        out_specs=pl.BlockSpec((rows, tn), lambda j: (0, j)),
        out_shape=jax.ShapeDtypeStruct((rows, n6), F32),
        compiler_params=_cparams(("arbitrary",)),
        name="ada",
    )(c_pad, w, b)


def _inproj_kernel(x_ref, mod_ref, w_ref, o_ref, u_ref):
    @pl.when(pl.program_id(1) == 0)
    def _():
        m = mod_ref[0]
        u = _ln(x_ref[...]) * (1.0 + m[1:2]) + m[0:1]
        u_ref[...] = u.astype(BF16)

    o_ref[...] = jnp.dot(u_ref[...], w_ref[...], preferred_element_type=F32)


def _inproj(x, mod, w, seq, tm, tn):
    n, d = x.shape
    nout = w.shape[1]
    return pl.pallas_call(
        _inproj_kernel,
        grid=(n // tm, nout // tn),
        in_specs=[
            pl.BlockSpec((tm, d), lambda i, j: (i, 0)),
            pl.BlockSpec((1, 6, d), lambda i, j: (i * tm // seq, 0, 0)),
            pl.BlockSpec((d, tn), lambda i, j: (0, j)),
        ],
        out_specs=pl.BlockSpec((tm, tn), lambda i, j: (i, j)),
        out_shape=jax.ShapeDtypeStruct((n, nout), F32),
        scratch_shapes=[pltpu.VMEM((tm, d), BF16)],
        compiler_params=_cparams(("arbitrary", "arbitrary")),
        name="inproj",
    )(x, mod, w)


def _lru_kernel(xl_ref, gl_ref, cw_ref, cb_ref, wa_ref, ba_ref, wx_ref, bx_ref, lam_ref,
                o_ref, xe_ref, a_ref, b_ref, h_ref, *, nblk):
    tc, c = xl_ref.shape

    @pl.when(pl.program_id(1) == 0)
    def _():
        xe_ref[0:SUBLANES, :] = jnp.zeros((SUBLANES, c), F32)
        h_ref[...] = jnp.zeros_like(h_ref)

    x = xl_ref[...]
    xe_ref[SUBLANES:SUBLANES + tc, :] = x
    cw = cw_ref[...]
    xc = (x * cw[3:4]
          + xe_ref[pl.ds(SUBLANES - 1, tc), :] * cw[2:3]
          + xe_ref[pl.ds(SUBLANES - 2, tc), :] * cw[1:2]
          + xe_ref[pl.ds(SUBLANES - 3, tc), :] * cw[0:1]
          + cb_ref[...])
    xe_ref[0:SUBLANES, :] = x[tc - SUBLANES:tc, :]

    lam = lam_ref[...]
    z = -lam
    softplus = jnp.maximum(z, 0.0) + jnp.log1p(jnp.exp(-jnp.abs(z)))
    for nb in range(nblk):
        sl = slice(nb * LANES, (nb + 1) * LANES)
        xb = xc[:, sl]
        xb16 = xb.astype(BF16)
        r = jax.nn.sigmoid(jnp.dot(xb16, wa_ref[nb], preferred_element_type=F32) + ba_ref[:, sl])
        i = jax.nn.sigmoid(jnp.dot(xb16, wx_ref[nb], preferred_element_type=F32) + bx_ref[:, sl])
        log_a = (-LRU_C) * r * softplus[:, sl]
        a = jnp.exp(log_a)
        one_m_a2 = -jnp.tanh(log_a) * (a * a + 1.0)
        a_ref[:, sl] = a
        b_ref[:, sl] = jnp.sqrt(one_m_a2) * (i * xb)

    row = lax.broadcasted_iota(I32, (SUBLANES, c), 0)

    def body(g, h):
        r0 = pl.multiple_of(g * SUBLANES, SUBLANES)
        a = a_ref[pl.ds(r0, SUBLANES), :]
        b = b_ref[pl.ds(r0, SUBLANES), :]
        for d in (1, 2, 4):
            keep = row >= d
            a_s = pltpu.roll(a, d, 0)
            b_s = pltpu.roll(b, d, 0)
            b = jnp.where(keep, a * b_s + b, b)
            a = jnp.where(keep, a * a_s, a)
        hh = a * h + b
        o_ref[pl.ds(r0, SUBLANES), :] = (hh * _gelu(gl_ref[pl.ds(r0, SUBLANES), :])).astype(o_ref.dtype)
        return hh[SUBLANES - 1:SUBLANES, :]

    h_ref[0:1, :] = lax.fori_loop(0, tc // SUBLANES, body, h_ref[0:1, :])


def _lru(proj, cw, cb, wa, ba, wx, bx, lam, batch, seq, tc):
    n = proj.shape[0]
    c = cw.shape[1]
    nblk = wa.shape[0]
    nt = seq // tc
    full = lambda shape: pl.BlockSpec(shape, lambda b, t: (0,) * len(shape))
    return pl.pallas_call(
        functools.partial(_lru_kernel, nblk=nblk),
        grid=(batch, nt),
        in_specs=[
            pl.BlockSpec((tc, c), lambda b, t: (b * nt + t, 0)),
            pl.BlockSpec((tc, c), lambda b, t: (b * nt + t, 1)),
            full((CONV_W, c)), full((1, c)),
            full((nblk, LANES, LANES)), full((1, c)),
            full((nblk, LANES, LANES)), full((1, c)),
            full((1, c)),
        ],
        out_specs=pl.BlockSpec((tc, c), lambda b, t: (b * nt + t, 0)),
        out_shape=jax.ShapeDtypeStruct((n, c), BF16),
        scratch_shapes=[
            pltpu.VMEM((tc + SUBLANES, c), F32),
            pltpu.VMEM((tc, c), F32),
            pltpu.VMEM((tc, c), F32),
            pltpu.VMEM((SUBLANES, c), F32),
        ],
        compiler_params=_cparams(("arbitrary", "arbitrary")),
        name="lru",
    )(proj, proj, cw, cb, wa, ba, wx, bx, lam)


def _rope_tables(ang, lane_in_head, half):
    cos = jnp.cos(ang)
    sin = jnp.sin(ang)
    c = jnp.where(lane_in_head < 2 * half, cos, 1.0)
    s1 = jnp.where(lane_in_head < half, -sin, 0.0)
    s2 = jnp.where((lane_in_head >= half) & (lane_in_head < 2 * half), sin, 0.0)
    return c, s1, s2


def _rope(x, tabs, half):
    c, s1, s2 = tabs
    return x * c + pltpu.roll(x, LANES - half, 1) * s1 + pltpu.roll(x, half, 1) * s2


def _dsaprep_kernel(q_ref, k_ref, v_ref, qi_ref, kw_ref, pos_ref, g_ref, fa_ref, fi_ref,
                    qo_ref, ko_ref, vo_ref, qio_ref, kdo_ref, wo_ref, *, nh, npair, wscale):
    tp = q_ref.shape[0]
    pos = pos_ref[...].astype(F32)
    lane = lax.broadcasted_iota(I32, (tp, LANES), 1)
    tab_a = _rope_tables(pos * fa_ref[...], lane, ROT_ATT // 2)
    tab_i = _rope_tables(pos * fi_ref[...], lane % IDX_DIM, ROT_IDX // 2)
    for h in range(nh):
        sl = slice(h * LANES, (h + 1) * LANES)
        qo_ref[:, sl] = _rope(q_ref[:, sl], tab_a, ROT_ATT // 2).astype(BF16)
        ko_ref[:, sl] = _rope(k_ref[:, sl], tab_a, ROT_ATT // 2).astype(BF16)
    vo_ref[...] = v_ref[...].astype(BF16)
    first = lane < IDX_DIM
    for p in range(npair):
        r = _rope(qi_ref[:, p * LANES:(p + 1) * LANES], tab_i, ROT_IDX // 2)
        qio_ref[:, (2 * p) * LANES:(2 * p + 1) * LANES] = jnp.where(first, r, 0.0).astype(BF16)
        qio_ref[:, (2 * p + 1) * LANES:(2 * p + 2) * LANES] = jnp.where(first, 0.0, r).astype(BF16)
    kw = kw_ref[...]
    mu = jnp.sum(jnp.where(first, kw, 0.0), axis=1, keepdims=True) * (1.0 / IDX_DIM)
    kc = jnp.where(first, kw - mu, 0.0)
    var = jnp.sum(kc * kc, axis=1, keepdims=True) * (1.0 / IDX_DIM)
    kn = kc * lax.rsqrt(var + LN_EPS) * g_ref[...]
    kr = _rope(kn, tab_i, ROT_IDX // 2)
    kr = jnp.where(first, kr, 0.0)
    kdo_ref[...] = (kr + pltpu.roll(kr, IDX_DIM, 1)).astype(BF16)
    wo_ref[...] = kw * wscale


def _dsaprep(proj, pos, g_pad, fa, fi, d_lru, d_att, ni, tp):
    n = proj.shape[0]
    nh = d_att // HEAD_DIM
    npair = ni * IDX_DIM // LANES
    assert d_lru % d_att == 0 or d_att % d_lru == 0
    q0 = 2 * d_lru // d_att
    kw_blk = (2 * d_lru + 3 * d_att + ni * IDX_DIM) // LANES
    assert (ni * IDX_DIM) % d_att == 0 or ni * IDX_DIM == d_att
    wide = lambda j: pl.BlockSpec((tp, d_att), lambda i: (i, j))
    row1 = lambda: pl.BlockSpec((1, LANES), lambda i: (0, 0))
    out_w = lambda: pl.BlockSpec((tp, d_att), lambda i: (i, 0))
    out_n = lambda: pl.BlockSpec((tp, LANES), lambda i: (i, 0))
    wscale = float(ni) ** -0.5 * float(IDX_DIM) ** -0.5
    return pl.pallas_call(
        functools.partial(_dsaprep_kernel, nh=nh, npair=npair, wscale=wscale),
        grid=(n // tp,),
        in_specs=[
            wide(q0), wide(q0 + 1), wide(q0 + 2),
            pl.BlockSpec((tp, ni * IDX_DIM), lambda i: (i, (2 * d_lru + 3 * d_att) // (ni * IDX_DIM))),
            pl.BlockSpec((tp, LANES), lambda i: (i, kw_blk)),
            pl.BlockSpec((tp, 1), lambda i: (i, 0)),
            row1(), row1(), row1(),
        ],
        out_specs=[out_w(), out_w(), out_w(),
                   pl.BlockSpec((tp, ni * LANES), lambda i: (i, 0)), out_n(), out_n()],
        out_shape=[
            jax.ShapeDtypeStruct((n, d_att), BF16),
            jax.ShapeDtypeStruct((n, d_att), BF16),
            jax.ShapeDtypeStruct((n, d_att), BF16),
            jax.ShapeDtypeStruct((n, ni * LANES), BF16),
            jax.ShapeDtypeStruct((n, LANES), BF16),
            jax.ShapeDtypeStruct((n, LANES), F32),
        ],
        compiler_params=_cparams(("arbitrary",)),
        name="dsaprep",
    )(proj, proj, proj, proj, proj, pos, g_pad, fa, fi)


def _dsa_kernel(q_ref, qi_ref, w_ref, k_ref, v_ref, kd_ref, o_ref, sc_ref, wb_ref,
                m_ref, l_ref, acc_ref, *, tq, tk, nh, ni, topk, att_scale):
    qb = pl.program_id(1)
    ntile = ((qb + 1) * tq + tk - 1) // tk
    nsub = tk // LANES
    topk_f = float(topk)

    w = w_ref[...]
    for h in range(ni):
        wb_ref[h] = jnp.broadcast_to(w[:, IDX_DIM + h:IDX_DIM + h + 1], (tq, LANES))
    row = lax.broadcasted_iota(I32, (tq, tk), 0)
    lane = lax.broadcasted_iota(I32, (tq, tk), 1)
    bound = (((qb * tq + row) >> CHUNK_SHIFT) + 1) << CHUNK_SHIFT

    n_adm = bound[:, 0:1].astype(F32)

    def score_tile(j, carry):
        mx, mn = carry
        k0 = pl.multiple_of(j * tk, tk)
        kd = kd_ref[pl.ds(k0, tk), :]
        acc = [jnp.zeros((tq, LANES), F32) for _ in range(nsub)]
        for h in range(ni):
            qh = qi_ref[:, h * LANES:(h + 1) * LANES]
            logit = lax.dot_general(qh, kd, _NT, preferred_element_type=F32)
            wh = wb_ref[h]
            for s in range(nsub):
                acc[s] = acc[s] + wh * jnp.maximum(logit[:, s * LANES:(s + 1) * LANES], 0.0)
        score = jnp.concatenate(acc, axis=1)
        adm = (k0 + lane) < bound
        sc_ref[j] = jnp.where(adm, score, -jnp.inf)
        mx = jnp.maximum(mx, jnp.max(jnp.where(adm, score, -jnp.inf), axis=1, keepdims=True))
        mn = jnp.minimum(mn, jnp.min(jnp.where(adm, score, jnp.inf), axis=1, keepdims=True))
        return mx, mn

    mx, mn = lax.fori_loop(0, ntile, score_tile,
                           (jnp.full((tq, 1), -jnp.inf, F32), jnp.full((tq, 1), jnp.inf, F32)))

    def count_ge(mid):
        def body(j, acc):
            ge = jnp.where(sc_ref[j] >= mid, 1.0, 0.0)
            for s in range(nsub):
                acc = acc + ge[:, s * LANES:(s + 1) * LANES]
            return acc
        acc = lax.fori_loop(0, ntile, body, jnp.zeros((tq, LANES), F32))
        return jnp.sum(acc, axis=1, keepdims=True)

    small = n_adm <= topk_f

    def pending(resolved):
        return jnp.max(jnp.where(resolved, 0.0, 1.0))

    def bisect_cond(c):
        return (c[0] < MAX_BISECT) & (c[4] > 0.0)

    def bisect(c):
        it, lo, hi, cnt_lo, _ = c
        mid = 0.5 * lo + 0.5 * hi
        cnt = count_ge(mid)
        ge = cnt >= topk_f
        adjacent = (mid <= lo) | (mid >= hi)
        lo = jnp.where(ge, mid, lo)
        cnt_lo = jnp.where(ge, cnt, cnt_lo)
        hi = jnp.where(ge, hi, mid)
        return it + 1, lo, hi, cnt_lo, pending((cnt_lo == topk_f) | small | adjacent)

    _, thr, _, _, _ = lax.while_loop(
        bisect_cond, bisect, (jnp.int32(0), mn, mx, n_adm, pending((n_adm == topk_f) | small)))
    thr = jnp.where(small, -jnp.inf, thr)

    def bias_tile(j, carry):
        k0 = pl.multiple_of(j * tk, tk)
        sel = (sc_ref[j] >= thr) & ((k0 + lane) < bound)
        sc_ref[j] = jnp.where(sel, 0.0, NEG_BIG)
        return carry

    lax.fori_loop(0, ntile, bias_tile, 0)

    for h in range(nh):
        m_ref[h] = jnp.full((tq, LANES), NEG_BIG, F32)
        l_ref[h] = jnp.zeros((tq, LANES), F32)
        acc_ref[h] = jnp.zeros((tq, HEAD_DIM), F32)

    def att_tile(j, carry):
        k0 = pl.multiple_of(j * tk, tk)
        bias = sc_ref[j]
        heads_sl = [slice(h * HEAD_DIM, (h + 1) * HEAD_DIM) for h in range(nh)]
        qk = [lax.dot_general(q_ref[:, sl], k_ref[pl.ds(k0, tk), sl], _NT, preferred_element_type=F32)
              for sl in heads_sl]
        for h, sl in enumerate(heads_sl):
            vt = v_ref[pl.ds(k0, tk), sl]
            s = qk[h] * att_scale + bias
            m_old = m_ref[h]
            m_new = jnp.maximum(m_old, jnp.max(s, axis=1, keepdims=True))
            alpha = jnp.exp(m_old - m_new)
            p = jnp.exp(s - jnp.concatenate([m_new] * nsub, axis=1))
            l_ref[h] = alpha * l_ref[h] + jnp.sum(p, axis=1, keepdims=True)
            acc_ref[h] = alpha * acc_ref[h] + jnp.dot(p.astype(BF16), vt, preferred_element_type=F32)
            m_ref[h] = m_new
        return carry

    lax.fori_loop(0, ntile, att_tile, 0)
    for h in range(nh):
        sl = slice(h * HEAD_DIM, (h + 1) * HEAD_DIM)
        o_ref[:, sl] = (acc_ref[h] / l_ref[h]).astype(o_ref.dtype)


def _dsa(q, k, v, qi, kd, w, batch, seq, tq, tk):
    n, d_att = q.shape
    nh = d_att // HEAD_DIM
    ni = qi.shape[1] // LANES
    topk = min(TOPK_MAX, seq // 4)
    nq = seq // tq
    assert tq % CHUNK == 0 and seq % tq == 0 and seq % tk == 0 and tk % CHUNK == 0 and tk >= topk
    qspec = lambda width: pl.BlockSpec((tq, width), lambda b, i: (b * nq + i, 0))
    kspec = lambda width: pl.BlockSpec((seq, width), lambda b, i: (b, 0), pipeline_mode=pl.Buffered(1))
    return pl.pallas_call(
        functools.partial(_dsa_kernel, tq=tq, tk=tk, nh=nh, ni=ni, topk=topk,
                          att_scale=float(HEAD_DIM) ** -0.5),
        grid=(batch, nq),
        in_specs=[qspec(d_att), qspec(ni * LANES), qspec(LANES),
                  kspec(d_att), kspec(d_att), kspec(LANES)],
        out_specs=qspec(d_att),
        out_shape=jax.ShapeDtypeStruct((n, d_att), BF16),
        scratch_shapes=[
            pltpu.VMEM((seq // tk, tq, tk), F32),
            pltpu.VMEM((ni, tq, LANES), F32),
            pltpu.VMEM((nh, tq, LANES), F32),
            pltpu.VMEM((nh, tq, LANES), F32),
            pltpu.VMEM((nh, tq, HEAD_DIM), F32),
        ],
        compiler_params=_cparams(("arbitrary", "arbitrary")),
        name="dsa",
    )(q, qi, w, k, v, kd)


def _outproj_kernel(yl_ref, ya_ref, x_ref, mod_ref, wl_ref, wa_ref, g_ref, b_ref,
                    x1_ref, u2_ref, *, alpha):
    m = mod_ref[0]
    mix = (jnp.dot(yl_ref[...], wl_ref[...], preferred_element_type=F32)
           + jnp.dot(ya_ref[...], wa_ref[...], preferred_element_type=F32))
    x1 = _ln(alpha * x_ref[...] + (1.0 + m[2:3]) * mix) * g_ref[...] + b_ref[...]
    x1_ref[...] = x1
    u2_ref[...] = pltpu.bitcast((_ln(x1) * (1.0 + m[4:5]) + m[3:4]).astype(BF16), I32)


def _outproj(yl, ya, x, mod, wl, wa, g, b, seq, alpha, tm):
    n, d = x.shape
    row = lambda width: pl.BlockSpec((tm, width), lambda i: (i, 0))
    full = lambda shape: pl.BlockSpec(shape, lambda i: (0,) * len(shape))
    return pl.pallas_call(
        functools.partial(_outproj_kernel, alpha=alpha),
        grid=(n // tm,),
        in_specs=[row(yl.shape[1]), row(ya.shape[1]), row(d),
                  pl.BlockSpec((1, 6, d), lambda i: (i * tm // seq, 0, 0)),
                  full(wl.shape), full(wa.shape), full((1, d)), full((1, d))],
        out_specs=[row(d), pl.BlockSpec((tm // 2, d), lambda i: (i, 0))],
        out_shape=[jax.ShapeDtypeStruct((n, d), F32), jax.ShapeDtypeStruct((n // 2, d), I32)],
        compiler_params=_cparams(("arbitrary",)),
        name="outproj",
    )(yl, ya, x, mod, wl, wa, g, b)


def _topk_desc(x, k):
    out = []
    for _ in range(k):
        m = jnp.max(x, axis=0, keepdims=True)
        out.append(m)
        x = jnp.where(x == m, -jnp.inf, x)
    return out


def _route_kernel(u_ref, wpq_ref, sk_ref, th_ref, e1_ref, s2_ref, e2_ref,
                  v1_ref, v2_ref, cand_ref, *, heads):
    tm = 2 * u_ref.shape[0]
    nlt = tm // LANES
    k1 = PEER_TOPK + 1
    qp = jnp.dot(pltpu.bitcast(u_ref[...], BF16), wpq_ref[...], preferred_element_type=F32).astype(BF16)
    cand_ref[...] = jnp.full(cand_ref.shape, -jnp.inf, F32)
    for h in range(heads):
        st = []
        for p in range(2):
            col = (2 * h + p) * LANES
            st.append(lax.dot_general(sk_ref[h, p], qp[:, col:col + LANES], _NT,
                                      preferred_element_type=F32))
        s1, s2 = st
        for k, m in enumerate(_topk_desc(s1, k1)):
            v1_ref[k:k + 1, :] = m
        for k, m in enumerate(_topk_desc(s2, k1)):
            v2_ref[k:k + 1, :] = m
        off = 0
        for i in range(k1):
            nj = k1 // (i + 1)
            cand_ref[off:off + nj, :] = v1_ref[i:i + 1, :] + v2_ref[0:nj, :]
            off += nj
        m1 = v1_ref[0:1, :]
        m2 = v2_ref[0:1, :]
        top = _topk_desc(cand_ref[...], k1)
        tau = 0.5 * top[PEER_TOPK - 1] + 0.5 * top[PEER_TOPK]
        z = top[0] * 0.0
        for c in top[:PEER_TOPK]:
            z = z + jnp.exp(c - (m1 + m2))
        th = tau - s1
        e1 = jnp.exp(s1 - m1) / z
        e2 = jnp.exp(s2 - m2)
        for lt in range(nlt):
            ls = slice(lt * LANES, (lt + 1) * LANES)
            th_ref[h, lt] = th[:, ls]
            e1_ref[h, lt] = e1[:, ls]
            s2_ref[h, lt] = s2[:, ls]
            e2_ref[h, lt] = e2[:, ls]


def _route(u2, wpq, sk, tm):
    n, d = 2 * u2.shape[0], u2.shape[1]
    heads, _, nkeys, kd = sk.shape
    assert nkeys == LANES and kd == LANES
    nt = n // LANES
    nlt = tm // LANES
    k1 = PEER_TOPK + 1
    ncand = sum(k1 // (i + 1) for i in range(k1))
    ncand_pad = -(-ncand // SUBLANES) * SUBLANES
    list_rows = -(-k1 // SUBLANES) * SUBLANES
    tile = lambda: pl.BlockSpec((heads, nlt, nkeys, LANES), lambda i: (0, i, 0, 0))
    shp = jax.ShapeDtypeStruct((heads, nt, nkeys, LANES), F32)
    return pl.pallas_call(
        functools.partial(_route_kernel, heads=heads),
        grid=(n // tm,),
        in_specs=[
            pl.BlockSpec((tm // 2, d), lambda i: (i, 0)),
            pl.BlockSpec(wpq.shape, lambda i: (0, 0)),
            pl.BlockSpec(sk.shape, lambda i: (0, 0, 0, 0)),
        ],
        out_specs=[tile(), tile(), tile(), tile()],
        out_shape=[shp, shp, shp, shp],
        scratch_shapes=[
            pltpu.VMEM((list_rows, tm), F32),
            pltpu.VMEM((list_rows, tm), F32),
            pltpu.VMEM((ncand_pad, tm), F32),
        ],
        compiler_params=_cparams(("arbitrary",)),
        name="route",
    )(u2, wpq, sk)


def _gelu_times(x, g):
    c = math.sqrt(2.0 / math.pi)
    t = jnp.tanh(x * (c + (c * 0.044715) * (x * x)))
    return (x * g) * (0.5 * t + 0.5)


def _peer_kernel(u_ref, ut_ref, vt_ref, th_ref, e1_ref, s2_ref, e2_ref,
                 o_ref, act_ref, hk_ref, *, heads, na):
    tm = 2 * u_ref.shape[0]
    nlt = tm // LANES

    @pl.when(pl.program_id(1) == 0)
    def _():
        o_ref[...] = jnp.zeros_like(o_ref)

    act_ref[...] = lax.dot_general(ut_ref[...], pltpu.bitcast(u_ref[...], BF16), _NT,
                                   preferred_element_type=F32)

    def gate_rows(a, carry):
        r0 = pl.multiple_of(a * LANES, LANES)
        for lt in range(nlt):
            ls = slice(lt * LANES, (lt + 1) * LANES)
            g = jnp.zeros((LANES, LANES), F32)
            for h in range(heads):
                th = th_ref[h, lt, pl.ds(a, 1), :]
                e1 = e1_ref[h, lt, pl.ds(a, 1), :]
                g = g + jnp.where(s2_ref[h, lt] >= th, e1 * e2_ref[h, lt], 0.0)
            hk_ref[pl.ds(r0, LANES), ls] = _gelu_times(act_ref[pl.ds(r0, LANES), ls], g).astype(BF16)
        return carry

    lax.fori_loop(0, na, gate_rows, 0)
    o_ref[...] += lax.dot_general(hk_ref[...], vt_ref[...], _TN, preferred_element_type=F32)


def _peer(u2, ut, vt, th, e1, s2, e2, tm, te):
    n, d = 2 * u2.shape[0], u2.shape[1]
    ne = vt.shape[0]
    heads = th.shape[0]
    nlt = tm // LANES
    na = te // LANES
    assert na == SUBLANES
    tile = lambda rows: pl.BlockSpec((heads, nlt, rows, LANES), lambda i, j: (0, i, 0, 0),
                                     pipeline_mode=pl.Buffered(1))
    arow = lambda: pl.BlockSpec((heads, nlt, na, LANES), lambda i, j: (0, i, j, 0))
    return pl.pallas_call(
        functools.partial(_peer_kernel, heads=heads, na=na),
        grid=(n // tm, ne // te),
        in_specs=[
            pl.BlockSpec((tm // 2, d), lambda i, j: (i, 0)),
            pl.BlockSpec((te, d), lambda i, j: (j, 0)),
            pl.BlockSpec((te, d), lambda i, j: (j, 0)),
            arow(), arow(), tile(LANES), tile(LANES),
        ],
        out_specs=pl.BlockSpec((tm, d), lambda i, j: (i, 0)),
        out_shape=jax.ShapeDtypeStruct((n, d), F32),
        scratch_shapes=[pltpu.VMEM((te, tm), F32), pltpu.VMEM((te, tm), BF16)],
        compiler_params=_cparams(("arbitrary", "arbitrary")),
        name="peer",
    )(u2, ut, vt, th, e1, s2, e2)


def _final_kernel(x1_ref, y_ref, mod_ref, g_ref, b_ref, o_ref, *, alpha):
    m = mod_ref[0]
    o_ref[...] = _ln(alpha * x1_ref[...] + (1.0 + m[5:6]) * y_ref[...]) * g_ref[...] + b_ref[...]


def _final(x1, y, mod, g, b, seq, alpha, tm):
    n, d = x1.shape
    row = lambda: pl.BlockSpec((tm, d), lambda i: (i, 0))
    full = lambda: pl.BlockSpec((1, d), lambda i: (0, 0))
    return pl.pallas_call(
        functools.partial(_final_kernel, alpha=alpha),
        grid=(n // tm,),
        in_specs=[row(), row(), pl.BlockSpec((1, 6, d), lambda i: (i * tm // seq, 0, 0)),
                  full(), full()],
        out_specs=row(),
        out_shape=jax.ShapeDtypeStruct((n, d), F32),
        compiler_params=_cparams(("arbitrary",)),
        name="final",
    )(x1, y, mod, g, b)


def _tile(total, want):
    t = min(total, want)
    while total % t:
        t //= 2
    return t


def _freq_row(rot, period):
    half = rot // 2
    inv = ROPE_THETA ** (-jnp.arange(half, dtype=F32) / half)
    lane = np.arange(LANES) % period
    return jnp.where(lane < rot, jnp.tile(inv, LANES // half), 0.0)[None, :]


def kernel(x, c, positions, w_ada, b_ada, w_in, conv_w, conv_b, w_rg_a, b_rg_a, w_rg_x, b_rg_x,
           lru_lambda, k_idx_g, w_out, ln1_g, ln1_b, w_pq, peer_sub_keys, peer_u, peer_v,
           ln2_g, ln2_b):
    batch, seq, d = x.shape
    depth = w_ada.shape[0]
    n = batch * seq
    d_lru = conv_w.shape[-1]
    d_att = w_out.shape[1] - d_lru
    d_in = w_in.shape[-1]
    ni = (d_in - 2 * d_lru - 3 * d_att - IDX_DIM) // (IDX_DIM + 1)
    assert 2 * d_lru + 3 * d_att + ni * IDX_DIM + IDX_DIM + ni == d_in
    alpha = (2.0 * depth) ** 0.25

    nout = -(-d_in // (2 * LANES)) * (2 * LANES)
    tn_in = _tile(nout, 1280) if nout % 1280 == 0 else 2 * LANES
    pos = positions.reshape(n, 1).astype(I32)
    fa = _freq_row(ROT_ATT, HEAD_DIM)
    fi = _freq_row(ROT_IDX, IDX_DIM)
    c_pad = jnp.pad(c, ((0, SUBLANES - batch), (0, 0)))

    xf = x.reshape(n, d)
    for l in range(depth):
        mod = _ada(c_pad, w_ada[l], b_ada[l][None, :])[:batch].reshape(batch, 6, d)
        w_in_p = jnp.pad(w_in[l], ((0, 0), (0, nout - d_in))).astype(BF16)
        proj = _inproj(xf, mod, w_in_p, seq, _tile(seq, 1024), tn_in)
        y_lru = _lru(proj, conv_w[l], conv_b[l][None, :], w_rg_a[l].astype(BF16), b_rg_a[l][None, :],
                     w_rg_x[l].astype(BF16), b_rg_x[l][None, :], lru_lambda[l][None, :],
                     batch, seq, _tile(seq, 256))
        g_pad = jnp.pad(k_idx_g[l], (0, LANES - IDX_DIM))[None, :]
        q, k, v, qi, kd, w = _dsaprep(proj, pos, g_pad, fa, fi, d_lru, d_att, ni, _tile(seq, 512))
        y_att = _dsa(q, k, v, qi, kd, w, batch, seq, _tile(seq, 4 * Q_BLOCK), _tile(seq, 512))
        wo = w_out[l].astype(BF16)
        x1, u2 = _outproj(y_lru, y_att, xf, mod, wo[:d_lru], wo[d_lru:], ln1_g[l][None, :],
                          ln1_b[l][None, :], seq, alpha, _tile(seq, 256))
        th, e1, s2, e2 = _route(u2, w_pq[l].astype(BF16), peer_sub_keys[l].astype(BF16),
                                _tile(seq, 256))
        y = _peer(u2, peer_u[l].astype(BF16), peer_v[l].astype(BF16), th, e1, s2, e2,
                  _tile(seq, 1024), SUBLANES * LANES)
        xf = _final(x1, y, mod, ln2_g[l][None, :], ln2_b[l][None, :], seq, alpha, _tile(seq, 512))
    return xf.reshape(batch, seq, d)
```

```python
import functools
import math

import jax
import jax.numpy as jnp
import numpy as np
from jax import lax
from jax.experimental import pallas as pl
from jax.experimental.pallas import tpu as pltpu

F32 = jnp.float32
BF16 = jnp.bfloat16
I32 = jnp.int32

LANES = 128
SUBLANES = 8
VMEM_LIMIT = 60 * 1024 * 1024

LN_EPS = 1e-5
CHUNK = 64
CHUNK_SHIFT = 6
assert 1 << CHUNK_SHIFT == CHUNK
CONV_W = 4
LRU_C = 8.0
HEAD_DIM = 128
IDX_DIM = 64
TOPK_MAX = 256
Q_BLOCK = 128
ROPE_THETA = 500000.0
ROT_ATT = HEAD_DIM // 4
ROT_IDX = IDX_DIM // 4
PEER_TOPK = 16
NEG_BIG = -1e30

MAX_BISECT = 300

_NT = (((1,), (1,)), ((), ()))
_TN = (((0,), (0,)), ((), ()))


def _cparams(sem):
    return pltpu.CompilerParams(dimension_semantics=sem, vmem_limit_bytes=VMEM_LIMIT)


def _ln(x):
    mu = jnp.mean(x, axis=-1, keepdims=True)
    xc = x - mu
    var = jnp.mean(xc * xc, axis=-1, keepdims=True)
    return xc * lax.rsqrt(var + LN_EPS)


def _gelu(x):
    return 0.5 * x * (1.0 + jnp.tanh(math.sqrt(2.0 / math.pi) * (x + 0.044715 * (x * x * x))))


def _ada_kernel(c_ref, w_ref, b_ref, o_ref):
    c = c_ref[...]
    s = c * jax.nn.sigmoid(c)
    o_ref[...] = jnp.dot(s, w_ref[...], preferred_element_type=F32) + b_ref[...]


def _ada(c_pad, w, b, tn=512):
    rows, d = c_pad.shape
    n6 = w.shape[1]
    return pl.pallas_call(
        _ada_kernel,
        grid=(n6 // tn,),
        in_specs=[
            pl.BlockSpec((rows, d), lambda j: (0, 0)),
            pl.BlockSpec((d, tn), lambda j: (0, j)),
            pl.BlockSpec((1, tn), lambda j: (0, j)),
        ],
        out_specs=pl.BlockSpec((rows, tn), lambda j: (0, j)),
        out_shape=jax.ShapeDtypeStruct((rows, n6), F32),
        compiler_params=_cparams(("arbitrary",)),
        name="ada",
    )(c_pad, w, b)


def _inproj_kernel(x_ref, mod_ref, w_ref, o_ref, u_ref):
    @pl.when(pl.program_id(1) == 0)
    def _():
        m = mod_ref[0]
        u = _ln(x_ref[...]) * (1.0 + m[1:2]) + m[0:1]
        u_ref[...] = u.astype(BF16)

    o_ref[...] = jnp.dot(u_ref[...], w_ref[...], preferred_element_type=F32)


def _inproj(x, mod, w, seq, tm, tn):
    n, d = x.shape
    nout = w.shape[1]
    return pl.pallas_call(
        _inproj_kernel,
        grid=(n // tm, nout // tn),
        in_specs=[
            pl.BlockSpec((tm, d), lambda i, j: (i, 0)),
            pl.BlockSpec((1, 6, d), lambda i, j: (i * tm // seq, 0, 0)),
            pl.BlockSpec((d, tn), lambda i, j: (0, j)),
        ],
        out_specs=pl.BlockSpec((tm, tn), lambda i, j: (i, j)),
        out_shape=jax.ShapeDtypeStruct((n, nout), F32),
        scratch_shapes=[pltpu.VMEM((tm, d), BF16)],
        compiler_params=_cparams(("arbitrary", "arbitrary")),
        name="inproj",
    )(x, mod, w)


def _lru_kernel(xl_ref, gl_ref, cw_ref, cb_ref, wa_ref, ba_ref, wx_ref, bx_ref, lam_ref,
                o_ref, xe_ref, a_ref, b_ref, h_ref, *, nblk):
    tc, c = xl_ref.shape

    @pl.when(pl.program_id(1) == 0)
    def _():
        xe_ref[0:SUBLANES, :] = jnp.zeros((SUBLANES, c), F32)
        h_ref[...] = jnp.zeros_like(h_ref)

    x = xl_ref[...]
    xe_ref[SUBLANES:SUBLANES + tc, :] = x
    cw = cw_ref[...]
    xc = (x * cw[3:4]
          + xe_ref[pl.ds(SUBLANES - 1, tc), :] * cw[2:3]
          + xe_ref[pl.ds(SUBLANES - 2, tc), :] * cw[1:2]
          + xe_ref[pl.ds(SUBLANES - 3, tc), :] * cw[0:1]
          + cb_ref[...])
    xe_ref[0:SUBLANES, :] = x[tc - SUBLANES:tc, :]

    lam = lam_ref[...]
    z = -lam
    softplus = jnp.maximum(z, 0.0) + jnp.log1p(jnp.exp(-jnp.abs(z)))
    for nb in range(nblk):
        sl = slice(nb * LANES, (nb + 1) * LANES)
        xb = xc[:, sl]
        xb16 = xb.astype(BF16)
        r = jax.nn.sigmoid(jnp.dot(xb16, wa_ref[nb], preferred_element_type=F32) + ba_ref[:, sl])
        i = jax.nn.sigmoid(jnp.dot(xb16, wx_ref[nb], preferred_element_type=F32) + bx_ref[:, sl])
        log_a = (-LRU_C) * r * softplus[:, sl]
        a = jnp.exp(log_a)
        one_m_a2 = -jnp.tanh(log_a) * (a * a + 1.0)
        a_ref[:, sl] = a
        b_ref[:, sl] = jnp.sqrt(one_m_a2) * (i * xb)

    row = lax.broadcasted_iota(I32, (SUBLANES, c), 0)

    def body(g, h):
        r0 = pl.multiple_of(g * SUBLANES, SUBLANES)
        a = a_ref[pl.ds(r0, SUBLANES), :]
        b = b_ref[pl.ds(r0, SUBLANES), :]
        for d in (1, 2, 4):
            keep = row >= d
            a_s = pltpu.roll(a, d, 0)
            b_s = pltpu.roll(b, d, 0)
            b = jnp.where(keep, a * b_s + b, b)
            a = jnp.where(keep, a * a_s, a)
        hh = a * h + b
        o_ref[pl.ds(r0, SUBLANES), :] = (hh * _gelu(gl_ref[pl.ds(r0, SUBLANES), :])).astype(o_ref.dtype)
        return hh[SUBLANES - 1:SUBLANES, :]

    h_ref[0:1, :] = lax.fori_loop(0, tc // SUBLANES, body, h_ref[0:1, :])


def _lru(proj, cw, cb, wa, ba, wx, bx, lam, batch, seq, tc):
    n = proj.shape[0]
    c = cw.shape[1]
    nblk = wa.shape[0]
    nt = seq // tc
    full = lambda shape: pl.BlockSpec(shape, lambda b, t: (0,) * len(shape))
    return pl.pallas_call(
        functools.partial(_lru_kernel, nblk=nblk),
        grid=(batch, nt),
        in_specs=[
            pl.BlockSpec((tc, c), lambda b, t: (b * nt + t, 0)),
            pl.BlockSpec((tc, c), lambda b, t: (b * nt + t, 1)),
            full((CONV_W, c)), full((1, c)),
            full((nblk, LANES, LANES)), full((1, c)),
            full((nblk, LANES, LANES)), full((1, c)),
            full((1, c)),
        ],
        out_specs=pl.BlockSpec((tc, c), lambda b, t: (b * nt + t, 0)),
        out_shape=jax.ShapeDtypeStruct((n, c), BF16),
        scratch_shapes=[
            pltpu.VMEM((tc + SUBLANES, c), F32),
            pltpu.VMEM((tc, c), F32),
            pltpu.VMEM((tc, c), F32),
            pltpu.VMEM((SUBLANES, c), F32),
        ],
        compiler_params=_cparams(("arbitrary", "arbitrary")),
        name="lru",
    )(proj, proj, cw, cb, wa, ba, wx, bx, lam)


def _rope_tables(ang, lane_in_head, half):
    cos = jnp.cos(ang)
    sin = jnp.sin(ang)
    c = jnp.where(lane_in_head < 2 * half, cos, 1.0)
    s1 = jnp.where(lane_in_head < half, -sin, 0.0)
    s2 = jnp.where((lane_in_head >= half) & (lane_in_head < 2 * half), sin, 0.0)
    return c, s1, s2


def _rope(x, tabs, half):
    c, s1, s2 = tabs
    return x * c + pltpu.roll(x, LANES - half, 1) * s1 + pltpu.roll(x, half, 1) * s2


def _dsaprep_kernel(q_ref, k_ref, v_ref, qi_ref, kw_ref, pos_ref, g_ref, fa_ref, fi_ref,
                    qo_ref, ko_ref, vo_ref, qio_ref, kdo_ref, wo_ref, *, nh, npair, wscale):
    tp = q_ref.shape[0]
    pos = pos_ref[...].astype(F32)
    lane = lax.broadcasted_iota(I32, (tp, LANES), 1)
    tab_a = _rope_tables(pos * fa_ref[...], lane, ROT_ATT // 2)
    tab_i = _rope_tables(pos * fi_ref[...], lane % IDX_DIM, ROT_IDX // 2)
    for h in range(nh):
        sl = slice(h * LANES, (h + 1) * LANES)
        qo_ref[:, sl] = _rope(q_ref[:, sl], tab_a, ROT_ATT // 2).astype(BF16)
        ko_ref[:, sl] = _rope(k_ref[:, sl], tab_a, ROT_ATT // 2).astype(BF16)
    vo_ref[...] = v_ref[...].astype(BF16)
    first = lane < IDX_DIM
    for p in range(npair):
        r = _rope(qi_ref[:, p * LANES:(p + 1) * LANES], tab_i, ROT_IDX // 2)
        qio_ref[:, (2 * p) * LANES:(2 * p + 1) * LANES] = jnp.where(first, r, 0.0).astype(BF16)
        qio_ref[:, (2 * p + 1) * LANES:(2 * p + 2) * LANES] = jnp.where(first, 0.0, r).astype(BF16)
    kw = kw_ref[...]
    mu = jnp.sum(jnp.where(first, kw, 0.0), axis=1, keepdims=True) * (1.0 / IDX_DIM)
    kc = jnp.where(first, kw - mu, 0.0)
    var = jnp.sum(kc * kc, axis=1, keepdims=True) * (1.0 / IDX_DIM)
    kn = kc * lax.rsqrt(var + LN_EPS) * g_ref[...]
    kr = _rope(kn, tab_i, ROT_IDX // 2)
    kr = jnp.where(first, kr, 0.0)
    kdo_ref[...] = (kr + pltpu.roll(kr, IDX_DIM, 1)).astype(BF16)
    wo_ref[...] = kw * wscale


def _dsaprep(proj, pos, g_pad, fa, fi, d_lru, d_att, ni, tp):
    n = proj.shape[0]
    nh = d_att // HEAD_DIM
    npair = ni * IDX_DIM // LANES
    assert d_lru % d_att == 0 or d_att % d_lru == 0
    q0 = 2 * d_lru // d_att
    kw_blk = (2 * d_lru + 3 * d_att + ni * IDX_DIM) // LANES
    assert (ni * IDX_DIM) % d_att == 0 or ni * IDX_DIM == d_att
    wide = lambda j: pl.BlockSpec((tp, d_att), lambda i: (i, j))
    row1 = lambda: pl.BlockSpec((1, LANES), lambda i: (0, 0))
    out_w = lambda: pl.BlockSpec((tp, d_att), lambda i: (i, 0))
    out_n = lambda: pl.BlockSpec((tp, LANES), lambda i: (i, 0))
    wscale = float(ni) ** -0.5 * float(IDX_DIM) ** -0.5
    return pl.pallas_call(
        functools.partial(_dsaprep_kernel, nh=nh, npair=npair, wscale=wscale),
        grid=(n // tp,),
        in_specs=[
            wide(q0), wide(q0 + 1), wide(q0 + 2),
            pl.BlockSpec((tp, ni * IDX_DIM), lambda i: (i, (2 * d_lru + 3 * d_att) // (ni * IDX_DIM))),
            pl.BlockSpec((tp, LANES), lambda i: (i, kw_blk)),
            pl.BlockSpec((tp, 1), lambda i: (i, 0)),
            row1(), row1(), row1(),
        ],
        out_specs=[out_w(), out_w(), out_w(),
                   pl.BlockSpec((tp, ni * LANES), lambda i: (i, 0)), out_n(), out_n()],
        out_shape=[
            jax.ShapeDtypeStruct((n, d_att), BF16),
            jax.ShapeDtypeStruct((n, d_att), BF16),
            jax.ShapeDtypeStruct((n, d_att), BF16),
            jax.ShapeDtypeStruct((n, ni * LANES), BF16),
            jax.ShapeDtypeStruct((n, LANES), BF16),
            jax.ShapeDtypeStruct((n, LANES), F32),
        ],
        compiler_params=_cparams(("arbitrary",)),
        name="dsaprep",
    )(proj, proj, proj, proj, proj, pos, g_pad, fa, fi)


def _dsa_kernel(q_ref, qi_ref, w_ref, k_ref, v_ref, kd_ref, o_ref, sc_ref, wb_ref,
                m_ref, l_ref, acc_ref, *, tq, tk, nh, ni, topk, att_scale):
    qb = pl.program_id(1)
    ntile = ((qb + 1) * tq + tk - 1) // tk
    nsub = tk // LANES
    topk_f = float(topk)

    w = w_ref[...]
    for h in range(ni):
        wb_ref[h] = jnp.broadcast_to(w[:, IDX_DIM + h:IDX_DIM + h + 1], (tq, LANES))
    row = lax.broadcasted_iota(I32, (tq, tk), 0)
    lane = lax.broadcasted_iota(I32, (tq, tk), 1)
    bound = (((qb * tq + row) >> CHUNK_SHIFT) + 1) << CHUNK_SHIFT

    n_adm = bound[:, 0:1].astype(F32)

    def score_tile(j, carry):
        mx, mn = carry
        k0 = pl.multiple_of(j * tk, tk)
        kd = kd_ref[pl.ds(k0, tk), :]
        acc = [jnp.zeros((tq, LANES), F32) for _ in range(nsub)]
        for h in range(ni):
            qh = qi_ref[:, h * LANES:(h + 1) * LANES]
            logit = lax.dot_general(qh, kd, _NT, preferred_element_type=F32)
            wh = wb_ref[h]
            for s in range(nsub):
                acc[s] = acc[s] + wh * jnp.maximum(logit[:, s * LANES:(s + 1) * LANES], 0.0)
        score = jnp.concatenate(acc, axis=1)
        adm = (k0 + lane) < bound
        sc_ref[j] = jnp.where(adm, score, -jnp.inf)
        mx = jnp.maximum(mx, jnp.max(jnp.where(adm, score, -jnp.inf), axis=1, keepdims=True))
        mn = jnp.minimum(mn, jnp.min(jnp.where(adm, score, jnp.inf), axis=1, keepdims=True))
        return mx, mn

    mx, mn = lax.fori_loop(0, ntile, score_tile,
                           (jnp.full((tq, 1), -jnp.inf, F32), jnp.full((tq, 1), jnp.inf, F32)))

    def count_ge(mid):
        def body(j, acc):
            ge = jnp.where(sc_ref[j] >= mid, 1.0, 0.0)
            for s in range(nsub):
                acc = acc + ge[:, s * LANES:(s + 1) * LANES]
            return acc
        acc = lax.fori_loop(0, ntile, body, jnp.zeros((tq, LANES), F32))
        return jnp.sum(acc, axis=1, keepdims=True)

    small = n_adm <= topk_f

    def pending(resolved):
        return jnp.max(jnp.where(resolved, 0.0, 1.0))

    def bisect_cond(c):
        return (c[0] < MAX_BISECT) & (c[4] > 0.0)

    def bisect(c):
        it, lo, hi, cnt_lo, _ = c
        mid = 0.5 * lo + 0.5 * hi
        cnt = count_ge(mid)
        ge = cnt >= topk_f
        adjacent = (mid <= lo) | (mid >= hi)
        lo = jnp.where(ge, mid, lo)
        cnt_lo = jnp.where(ge, cnt, cnt_lo)
        hi = jnp.where(ge, hi, mid)
        return it + 1, lo, hi, cnt_lo, pending((cnt_lo == topk_f) | small | adjacent)

    _, thr, _, _, _ = lax.while_loop(
        bisect_cond, bisect, (jnp.int32(0), mn, mx, n_adm, pending((n_adm == topk_f) | small)))
    thr = jnp.where(small, -jnp.inf, thr)

    def bias_tile(j, carry):
        k0 = pl.multiple_of(j * tk, tk)
        sel = (sc_ref[j] >= thr) & ((k0 + lane) < bound)
        sc_ref[j] = jnp.where(sel, 0.0, NEG_BIG)
        return carry

    lax.fori_loop(0, ntile, bias_tile, 0)

    for h in range(nh):
        m_ref[h] = jnp.full((tq, LANES), NEG_BIG, F32)
        l_ref[h] = jnp.zeros((tq, LANES), F32)
        acc_ref[h] = jnp.zeros((tq, HEAD_DIM), F32)

    def att_tile(j, carry):
        k0 = pl.multiple_of(j * tk, tk)
        bias = sc_ref[j]
        heads_sl = [slice(h * HEAD_DIM, (h + 1) * HEAD_DIM) for h in range(nh)]
        qk = [lax.dot_general(q_ref[:, sl], k_ref[pl.ds(k0, tk), sl], _NT, preferred_element_type=F32)
              for sl in heads_sl]
        for h, sl in enumerate(heads_sl):
            vt = v_ref[pl.ds(k0, tk), sl]
            s = qk[h] * att_scale + bias
            m_old = m_ref[h]
            m_new = jnp.maximum(m_old, jnp.max(s, axis=1, keepdims=True))
            alpha = jnp.exp(m_old - m_new)
            p = jnp.exp(s - jnp.concatenate([m_new] * nsub, axis=1))
            l_ref[h] = alpha * l_ref[h] + jnp.sum(p, axis=1, keepdims=True)
            acc_ref[h] = alpha * acc_ref[h] + jnp.dot(p.astype(BF16), vt, preferred_element_type=F32)
            m_ref[h] = m_new
        return carry

    lax.fori_loop(0, ntile, att_tile, 0)
    for h in range(nh):
        sl = slice(h * HEAD_DIM, (h + 1) * HEAD_DIM)
        o_ref[:, sl] = (acc_ref[h] / l_ref[h]).astype(o_ref.dtype)


def _dsa(q, k, v, qi, kd, w, batch, seq, tq, tk):
    n, d_att = q.shape
    nh = d_att // HEAD_DIM
    ni = qi.shape[1] // LANES
    topk = min(TOPK_MAX, seq // 4)
    nq = seq // tq
    assert tq % CHUNK == 0 and seq % tq == 0 and seq % tk == 0 and tk % CHUNK == 0 and tk >= topk
    qspec = lambda width: pl.BlockSpec((tq, width), lambda b, i: (b * nq + i, 0))
    kspec = lambda width: pl.BlockSpec((seq, width), lambda b, i: (b, 0), pipeline_mode=pl.Buffered(1))
    return pl.pallas_call(
        functools.partial(_dsa_kernel, tq=tq, tk=tk, nh=nh, ni=ni, topk=topk,
                          att_scale=float(HEAD_DIM) ** -0.5),
        grid=(batch, nq),
        in_specs=[qspec(d_att), qspec(ni * LANES), qspec(LANES),
                  kspec(d_att), kspec(d_att), kspec(LANES)],
        out_specs=qspec(d_att),
        out_shape=jax.ShapeDtypeStruct((n, d_att), BF16),
        scratch_shapes=[
            pltpu.VMEM((seq // tk, tq, tk), F32),
            pltpu.VMEM((ni, tq, LANES), F32),
            pltpu.VMEM((nh, tq, LANES), F32),
            pltpu.VMEM((nh, tq, LANES), F32),
            pltpu.VMEM((nh, tq, HEAD_DIM), F32),
        ],
        compiler_params=_cparams(("arbitrary", "arbitrary")),
        name="dsa",
    )(q, qi, w, k, v, kd)


def _outproj_kernel(yl_ref, ya_ref, x_ref, mod_ref, wl_ref, wa_ref, g_ref, b_ref,
                    x1_ref, u2_ref, *, alpha):
    m = mod_ref[0]
    mix = (jnp.dot(yl_ref[...], wl_ref[...], preferred_element_type=F32)
           + jnp.dot(ya_ref[...], wa_ref[...], preferred_element_type=F32))
    x1 = _ln(alpha * x_ref[...] + (1.0 + m[2:3]) * mix) * g_ref[...] + b_ref[...]
    x1_ref[...] = x1
    u2_ref[...] = pltpu.bitcast((_ln(x1) * (1.0 + m[4:5]) + m[3:4]).astype(BF16), I32)


def _outproj(yl, ya, x, mod, wl, wa, g, b, seq, alpha, tm):
    n, d = x.shape
    row = lambda width: pl.BlockSpec((tm, width), lambda i: (i, 0))
    full = lambda shape: pl.BlockSpec(shape, lambda i: (0,) * len(shape))
    return pl.pallas_call(
        functools.partial(_outproj_kernel, alpha=alpha),
        grid=(n // tm,),
        in_specs=[row(yl.shape[1]), row(ya.shape[1]), row(d),
                  pl.BlockSpec((1, 6, d), lambda i: (i * tm // seq, 0, 0)),
                  full(wl.shape), full(wa.shape), full((1, d)), full((1, d))],
        out_specs=[row(d), pl.BlockSpec((tm // 2, d), lambda i: (i, 0))],
        out_shape=[jax.ShapeDtypeStruct((n, d), F32), jax.ShapeDtypeStruct((n // 2, d), I32)],
        compiler_params=_cparams(("arbitrary",)),
        name="outproj",
    )(yl, ya, x, mod, wl, wa, g, b)


def _topk_desc(x, k):
    out = []
    for _ in range(k):
        m = jnp.max(x, axis=0, keepdims=True)
        out.append(m)
        x = jnp.where(x == m, -jnp.inf, x)
    return out


def _route_kernel(u_ref, wpq_ref, sk_ref, th_ref, e1_ref, s2_ref, e2_ref,
                  v1_ref, v2_ref, cand_ref, *, heads):
    tm = 2 * u_ref.shape[0]
    nlt = tm // LANES
    k1 = PEER_TOPK + 1
    qp = jnp.dot(pltpu.bitcast(u_ref[...], BF16), wpq_ref[...], preferred_element_type=F32).astype(BF16)
    cand_ref[...] = jnp.full(cand_ref.shape, -jnp.inf, F32)
    for h in range(heads):
        st = []
        for p in range(2):
            col = (2 * h + p) * LANES
            st.append(lax.dot_general(sk_ref[h, p], qp[:, col:col + LANES], _NT,
                                      preferred_element_type=F32))
        s1, s2 = st
        for k, m in enumerate(_topk_desc(s1, k1)):
            v1_ref[k:k + 1, :] = m
        for k, m in enumerate(_topk_desc(s2, k1)):
            v2_ref[k:k + 1, :] = m
        off = 0
        for i in range(k1):
            nj = k1 // (i + 1)
            cand_ref[off:off + nj, :] = v1_ref[i:i + 1, :] + v2_ref[0:nj, :]
            off += nj
        m1 = v1_ref[0:1, :]
        m2 = v2_ref[0:1, :]
        top = _topk_desc(cand_ref[...], k1)
        tau = 0.5 * top[PEER_TOPK - 1] + 0.5 * top[PEER_TOPK]
        z = top[0] * 0.0
        for c in top[:PEER_TOPK]:
            z = z + jnp.exp(c - (m1 + m2))
        th = tau - s1
        e1 = jnp.exp(s1 - m1) / z
        e2 = jnp.exp(s2 - m2)
        for lt in range(nlt):
            ls = slice(lt * LANES, (lt + 1) * LANES)
            th_ref[h, lt] = th[:, ls]
            e1_ref[h, lt] = e1[:, ls]
            s2_ref[h, lt] = s2[:, ls]
            e2_ref[h, lt] = e2[:, ls]


def _route(u2, wpq, sk, tm):
    n, d = 2 * u2.shape[0], u2.shape[1]
    heads, _, nkeys, kd = sk.shape
    assert nkeys == LANES and kd == LANES
    nt = n // LANES
    nlt = tm // LANES
    k1 = PEER_TOPK + 1
    ncand = sum(k1 // (i + 1) for i in range(k1))
    ncand_pad = -(-ncand // SUBLANES) * SUBLANES
    list_rows = -(-k1 // SUBLANES) * SUBLANES
    tile = lambda: pl.BlockSpec((heads, nlt, nkeys, LANES), lambda i: (0, i, 0, 0))
    shp = jax.ShapeDtypeStruct((heads, nt, nkeys, LANES), F32)
    return pl.pallas_call(
        functools.partial(_route_kernel, heads=heads),
        grid=(n // tm,),
        in_specs=[
            pl.BlockSpec((tm // 2, d), lambda i: (i, 0)),
            pl.BlockSpec(wpq.shape, lambda i: (0, 0)),
            pl.BlockSpec(sk.shape, lambda i: (0, 0, 0, 0)),
        ],
        out_specs=[tile(), tile(), tile(), tile()],
        out_shape=[shp, shp, shp, shp],
        scratch_shapes=[
            pltpu.VMEM((list_rows, tm), F32),
            pltpu.VMEM((list_rows, tm), F32),
            pltpu.VMEM((ncand_pad, tm), F32),
        ],
        compiler_params=_cparams(("arbitrary",)),
        name="route",
    )(u2, wpq, sk)


def _gelu_times(x, g):
    c = math.sqrt(2.0 / math.pi)
    t = jnp.tanh(x * (c + (c * 0.044715) * (x * x)))
    return (x * g) * (0.5 * t + 0.5)


def _peer_kernel(u_ref, ut_ref, vt_ref, th_ref, e1_ref, s2_ref, e2_ref,
                 o_ref, act_ref, hk_ref, *, heads, na):
    tm = 2 * u_ref.shape[0]
    nlt = tm // LANES

    @pl.when(pl.program_id(1) == 0)
    def _():
        o_ref[...] = jnp.zeros_like(o_ref)

    act_ref[...] = lax.dot_general(ut_ref[...], pltpu.bitcast(u_ref[...], BF16), _NT,
                                   preferred_element_type=F32)

    def gate_rows(a, carry):
        r0 = pl.multiple_of(a * LANES, LANES)
        for lt in range(nlt):
            ls = slice(lt * LANES, (lt + 1) * LANES)
            g = jnp.zeros((LANES, LANES), F32)
            for h in range(heads):
                th = th_ref[h, lt, pl.ds(a, 1), :]
                e1 = e1_ref[h, lt, pl.ds(a, 1), :]
                g = g + jnp.where(s2_ref[h, lt] >= th, e1 * e2_ref[h, lt], 0.0)
            hk_ref[pl.ds(r0, LANES), ls] = _gelu_times(act_ref[pl.ds(r0, LANES), ls], g).astype(BF16)
        return carry

    lax.fori_loop(0, na, gate_rows, 0)
    o_ref[...] += lax.dot_general(hk_ref[...], vt_ref[...], _TN, preferred_element_type=F32)


def _peer(u2, ut, vt, th, e1, s2, e2, tm, te):
    n, d = 2 * u2.shape[0], u2.shape[1]
    ne = vt.shape[0]
    heads = th.shape[0]
    nlt = tm // LANES
    na = te // LANES
    assert na == SUBLANES
    tile = lambda rows: pl.BlockSpec((heads, nlt, rows, LANES), lambda i, j: (0, i, 0, 0),
                                     pipeline_mode=pl.Buffered(1))
    arow = lambda: pl.BlockSpec((heads, nlt, na, LANES), lambda i, j: (0, i, j, 0))
    return pl.pallas_call(
        functools.partial(_peer_kernel, heads=heads, na=na),
        grid=(n // tm, ne // te),
        in_specs=[
            pl.BlockSpec((tm // 2, d), lambda i, j: (i, 0)),
            pl.BlockSpec((te, d), lambda i, j: (j, 0)),
            pl.BlockSpec((te, d), lambda i, j: (j, 0)),
            arow(), arow(), tile(LANES), tile(LANES),
        ],
        out_specs=pl.BlockSpec((tm, d), lambda i, j: (i, 0)),
        out_shape=jax.ShapeDtypeStruct((n, d), F32),
        scratch_shapes=[pltpu.VMEM((te, tm), F32), pltpu.VMEM((te, tm), BF16)],
        compiler_params=_cparams(("arbitrary", "arbitrary")),
        name="peer",
    )(u2, ut, vt, th, e1, s2, e2)


def _final_kernel(x1_ref, y_ref, mod_ref, g_ref, b_ref, o_ref, *, alpha):
    m = mod_ref[0]
    o_ref[...] = _ln(alpha * x1_ref[...] + (1.0 + m[5:6]) * y_ref[...]) * g_ref[...] + b_ref[...]


def _final(x1, y, mod, g, b, seq, alpha, tm):
    n, d = x1.shape
    row = lambda: pl.BlockSpec((tm, d), lambda i: (i, 0))
    full = lambda: pl.BlockSpec((1, d), lambda i: (0, 0))
    return pl.pallas_call(
        functools.partial(_final_kernel, alpha=alpha),
        grid=(n // tm,),
        in_specs=[row(), row(), pl.BlockSpec((1, 6, d), lambda i: (i * tm // seq, 0, 0)),
                  full(), full()],
        out_specs=row(),
        out_shape=jax.ShapeDtypeStruct((n, d), F32),
        compiler_params=_cparams(("arbitrary",)),
        name="final",
    )(x1, y, mod, g, b)


def _tile(total, want):
    t = min(total, want)
    while total % t:
        t //= 2
    return t


def _freq_row(rot, period):
    half = rot // 2
    inv = ROPE_THETA ** (-jnp.arange(half, dtype=F32) / half)
    lane = np.arange(LANES) % period
    return jnp.where(lane < rot, jnp.tile(inv, LANES // half), 0.0)[None, :]


def kernel(x, c, positions, w_ada, b_ada, w_in, conv_w, conv_b, w_rg_a, b_rg_a, w_rg_x, b_rg_x,
           lru_lambda, k_idx_g, w_out, ln1_g, ln1_b, w_pq, peer_sub_keys, peer_u, peer_v,
           ln2_g, ln2_b):
    batch, seq, d = x.shape
    depth = w_ada.shape[0]
    n = batch * seq
    d_lru = conv_w.shape[-1]
    d_att = w_out.shape[1] - d_lru
    d_in = w_in.shape[-1]
    ni = (d_in - 2 * d_lru - 3 * d_att - IDX_DIM) // (IDX_DIM + 1)
    assert 2 * d_lru + 3 * d_att + ni * IDX_DIM + IDX_DIM + ni == d_in
    alpha = (2.0 * depth) ** 0.25

    nout = -(-d_in // (2 * LANES)) * (2 * LANES)
    tn_in = _tile(nout, 1280) if nout % 1280 == 0 else 2 * LANES
    pos = positions.reshape(n, 1).astype(I32)
    fa = _freq_row(ROT_ATT, HEAD_DIM)
    fi = _freq_row(ROT_IDX, IDX_DIM)
    c_pad = jnp.pad(c, ((0, SUBLANES - batch), (0, 0)))

    xf = x.reshape(n, d)
    for l in range(depth):
        mod = _ada(c_pad, w_ada[l], b_ada[l][None, :])[:batch].reshape(batch, 6, d)
        w_in_p = jnp.concatenate([w_in[l].astype(BF16), jnp.zeros((d, nout - d_in), BF16)], axis=1)
        proj = _inproj(xf, mod, w_in_p, seq, _tile(seq, 1024), tn_in)
        y_lru = _lru(proj, conv_w[l], conv_b[l][None, :], w_rg_a[l].astype(BF16), b_rg_a[l][None, :],
                     w_rg_x[l].astype(BF16), b_rg_x[l][None, :], lru_lambda[l][None, :],
                     batch, seq, _tile(seq, 256))
        g_pad = jnp.pad(k_idx_g[l], (0, LANES - IDX_DIM))[None, :]
        q, k, v, qi, kd, w = _dsaprep(proj, pos, g_pad, fa, fi, d_lru, d_att, ni, _tile(seq, 512))
        y_att = _dsa(q, k, v, qi, kd, w, batch, seq, _tile(seq, 4 * Q_BLOCK), _tile(seq, 512))
        wo = w_out[l].astype(BF16)
        x1, u2 = _outproj(y_lru, y_att, xf, mod, wo[:d_lru], wo[d_lru:], ln1_g[l][None, :],
                          ln1_b[l][None, :], seq, alpha, _tile(seq, 256))
        th, e1, s2, e2 = _route(u2, w_pq[l].astype(BF16), peer_sub_keys[l].astype(BF16),
                                _tile(seq, 256))
        y = _peer(u2, peer_u[l].astype(BF16), peer_v[l].astype(BF16), th, e1, s2, e2,
                  _tile(seq, 1024), SUBLANES * LANES)
        xf = _final(x1, y, mod, ln2_g[l][None, :], ln2_b[l][None, :], seq, alpha, _tile(seq, 512))
    return xf.reshape(batch, seq, d)
```
